```python
import math
import jax, jax.numpy as jnp
from jax import lax
import numpy as np

D_MODEL = 2048
BATCH = 2
SEQ = 16384
DEPTH = 2

GRID_W = 64
CTX_LEN = 256
ROPE_THETA = 10000.0
NORM_EPS = 1e-6
Q_BLOCK = 128

CONV_W = 1024
CONV_K = 3
DA_HEADS = 8
DA_HEAD = 64
DA_V = 2 * DA_HEAD
DA_SCALE = DA_HEAD ** -0.5
MLA_HEADS = 8
MLA_NOPE = 128
MLA_ROPE = 64
MLA_V = 128
MLA_QK = MLA_NOPE + MLA_ROPE
Q_LORA = 512
KV_LORA = 512
MLA_SCALE = MLA_QK ** -0.5
D_FF = 4 * D_MODEL
N_BRANCH = 3

Q_SIZES = (CONV_W, CONV_W, CONV_W, DA_HEADS * 2 * DA_HEAD, Q_LORA, N_BRANCH * D_MODEL)
KV_SIZES = (DA_HEADS * 2 * DA_HEAD, DA_HEADS * DA_V, KV_LORA, MLA_ROPE)
Q_COLS = sum(Q_SIZES)
KV_COLS = sum(KV_SIZES)
IN_COLS = Q_COLS + KV_COLS

kernel_name = 'hybrid_conv_diffattn_mla_dit_block'


def _rmsnorm(x, g):
    xf = x.astype(jnp.float32)
    y = xf * lax.rsqrt(jnp.mean(xf * xf, axis=-1, keepdims=True) + NORM_EPS)
    return (y * g.astype(jnp.float32)).astype(x.dtype)


def _modulate(xn, shift, scale):
    return xn * (1 + scale) + shift


def _split(p, sizes):
    return jnp.split(p, np.cumsum(sizes)[:-1].tolist(), axis=-1)


def _rope_1d(x, pos):
    d = x.shape[-1]
    inv_freq = ROPE_THETA ** (-jnp.arange(0, d, 2, dtype=jnp.float32) / d)
    ang = pos.astype(jnp.float32)[:, None] * inv_freq[None, :]
    shape = (pos.shape[0],) + (1,) * (x.ndim - 3) + (d // 2,)
    cos = jnp.cos(ang).reshape(shape)
    sin = jnp.sin(ang).reshape(shape)
    x1, x2 = jnp.split(x.astype(jnp.float32), 2, axis=-1)
    return jnp.concatenate([x1 * cos - x2 * sin, x2 * cos + x1 * sin], axis=-1).astype(x.dtype)


def _rope_2d(x, rows, cols):
    xr, xc = jnp.split(x, 2, axis=-1)
    return jnp.concatenate([_rope_1d(xr, rows), _rope_1d(xc, cols)], axis=-1)


def _short_conv(u, w):
    return lax.conv_general_dilated(
        u, w[:, None, :], window_strides=(1,), padding=[(CONV_K // 2, CONV_K // 2)],
        dimension_numbers=('NWC', 'WIO', 'NWC'), feature_group_count=u.shape[-1])


def _sdpa(q, k, v):
    s = jnp.einsum('bqhd,bkhd->bhqk', q, k).astype(jnp.float32) * MLA_SCALE
    p = jax.nn.softmax(s, axis=-1)
    return jnp.einsum('bhqk,bkhd->bqhd', p.astype(v.dtype), v)


def _diff_sdpa(q1, q2, k1, k2, v, lam):
    s1 = jnp.einsum('bqhd,bkhd->bhqk', q1, k1).astype(jnp.float32) * DA_SCALE
    s2 = jnp.einsum('bqhd,bkhd->bhqk', q2, k2).astype(jnp.float32) * DA_SCALE
    p = jax.nn.softmax(s1, axis=-1) - lam * jax.nn.softmax(s2, axis=-1)
    return jnp.einsum('bhqk,bkhd->bqhd', p.astype(v.dtype), v)


def _sweep_blocks(fn, *qs):
    B, S = qs[0].shape[:2]
    nb = S // Q_BLOCK
    blocks = tuple(jnp.moveaxis(q.reshape((B, nb, Q_BLOCK) + q.shape[2:]), 1, 0) for q in qs)
    out = lax.map(lambda qb: fn(*qb), blocks)
    return jnp.moveaxis(out, 0, 1).reshape((B, S) + out.shape[3:])


def _direct(fn, *qs):
    return fn(*qs)


def _da_queries(q, q_norm, pos):
    B, S = q.shape[:2]
    q = _rmsnorm(q.reshape(B, S, DA_HEADS, 2, DA_HEAD), q_norm)
    if pos is not None:
        q = _rope_2d(q, *pos)
    return q[..., 0, :], q[..., 1, :]


def _da_keys(k, v, k_norm, pos):
    B, S = k.shape[:2]
    k = _rmsnorm(k.reshape(B, S, DA_HEADS, 2, DA_HEAD), k_norm)
    if pos is not None:
        k = _rope_2d(k, *pos)
    return k[..., 0, :], k[..., 1, :], v.reshape(B, S, DA_HEADS, DA_V)


def _mla_queries(cq, q_a_norm, w_q_b, q_norm, pos):
    B, S = cq.shape[:2]
    q = jnp.einsum('bsr,rc->bsc', _rmsnorm(cq, q_a_norm), w_q_b).reshape(B, S, MLA_HEADS, MLA_QK)
    q_nope = _rmsnorm(q[..., :MLA_NOPE], q_norm[:MLA_NOPE])
    q_rope = _rmsnorm(q[..., MLA_NOPE:], q_norm[MLA_NOPE:])
    if pos is not None:
        q_rope = _rope_2d(q_rope, *pos)
    return jnp.concatenate([q_nope, q_rope], axis=-1)


def _mla_keys(ckv, k_rope, kv_a_norm, w_kv_b, k_norm, pos):
    B, S = ckv.shape[:2]
    kv = jnp.einsum('bsr,rc->bsc', _rmsnorm(ckv, kv_a_norm), w_kv_b).reshape(B, S, MLA_HEADS, MLA_NOPE + MLA_V)
    k_nope, v = jnp.split(kv, [MLA_NOPE], axis=-1)
    k_nope = _rmsnorm(k_nope, k_norm[:MLA_NOPE])
    k_rope = _rmsnorm(k_rope, k_norm[MLA_NOPE:])[:, :, None, :]
    if pos is not None:
        k_rope = _rope_2d(k_rope, *pos)
    k = jnp.concatenate([k_nope, jnp.broadcast_to(k_rope, (B, S, MLA_HEADS, MLA_ROPE))], axis=-1)
    return k, v


def _stream_keys(kv_parts, lp, pos):
    da_k, da_v, ckv, k_rope = kv_parts
    k1, k2, v_da = _da_keys(da_k, da_v, lp['da_k_norm'], pos)
    k_mla, v_mla = _mla_keys(ckv, k_rope, lp['mla_kv_a_norm'], lp['w_kv_b'], lp['mla_k_norm'], pos)
    return (k1, k2, v_da, k_mla, v_mla)


def _mixer(q_parts, keys, lp, lam, lam_init, pos, sweep):
    conv_b, conv_c, conv_h, da_q, mla_cq, gates = q_parts
    k1, k2, v_da, k_mla, v_mla = keys
    B, S = conv_h.shape[:2]
    y_conv = conv_b * _short_conv(conv_c * conv_h, lp['conv_w'])
    q1, q2 = _da_queries(da_q, lp['da_q_norm'], pos)
    o_da = sweep(lambda a, b: _diff_sdpa(a, b, k1, k2, v_da, lam), q1, q2)
    o_da = _rmsnorm(o_da, lp['da_subln']) * (1.0 - lam_init)
    q_m = _mla_queries(mla_cq, lp['mla_q_a_norm'], lp['w_q_b'], lp['mla_q_norm'], pos)
    o_mla = sweep(lambda a: _sdpa(a, k_mla, v_mla), q_m)
    g_conv, g_da, g_mla = jnp.split(jax.nn.sigmoid(gates), N_BRANCH, axis=-1)
    y = (g_conv * jnp.einsum('bsc,cd->bsd', y_conv, lp['w_conv_out'])
         + g_da * jnp.einsum('bsc,cd->bsd', o_da.reshape(B, S, DA_HEADS * DA_V), lp['w_da_out'])
         + g_mla * jnp.einsum('bsc,cd->bsd', o_mla.reshape(B, S, MLA_HEADS * MLA_V), lp['w_mla_out']))
    return jnp.einsum('bsd,de->bse', y, lp['w_o'])


def _ffn(xn, w1, w2):
    hid = jnp.square(jax.nn.relu(jnp.einsum('bsd,df->bsf', xn, w1)))
    return jnp.einsum('bsf,fd->bsd', hid, w2)


def setup_inputs(seed: int = 0) -> dict:
    key = jax.random.key(seed)
    ks = iter(jax.random.split(key, 32))
    L, D = DEPTH, D_MODEL

    def nrm(shape):
        return jax.random.normal(next(ks), shape, jnp.float32)

    def w(shape, fan_in, scale=1.0):
        return nrm(shape) * (scale * fan_in ** -0.5)

    def gain(shape):
        return 1.0 + 0.02 * nrm(shape)

    return {
        'x': nrm((BATCH, SEQ, D)),
        'c': nrm((BATCH, D)),
        'ctx': nrm((BATCH, CTX_LEN, D)),
        'c_ctx': nrm((D,)),
        'w_mod': w((L, D, 6 * D), D, 0.5),
        'b_mod': 0.01 * nrm((L, 6 * D)),
        'norm1_w': gain((L, D)),
        'norm2_w': gain((L, D)),
        'w_in': w((L, D, IN_COLS), D),
        'conv_w': w((L, CONV_K, CONV_W), CONV_K),
        'da_q_norm': gain((L, DA_HEAD)),
        'da_k_norm': gain((L, DA_HEAD)),
        'da_lambda': 0.1 * nrm((L, 4, DA_HEAD)),
        'da_subln': gain((L, DA_V)),
        'mla_q_a_norm': gain((L, Q_LORA)),
        'w_q_b': w((L, Q_LORA, MLA_HEADS * MLA_QK), Q_LORA),
        'mla_kv_a_norm': gain((L, KV_LORA)),
        'w_kv_b': w((L, KV_LORA, MLA_HEADS * (MLA_NOPE + MLA_V)), KV_LORA),
        'mla_q_norm': gain((L, MLA_QK)),
        'mla_k_norm': gain((L, MLA_QK)),
        'w_conv_out': w((L, CONV_W, D), CONV_W),
        'w_da_out': w((L, DA_HEADS * DA_V, D), DA_HEADS * DA_V),
        'w_mla_out': w((L, MLA_HEADS * MLA_V, D), MLA_HEADS * MLA_V),
        'w_o': w((L, D, D), D),
        'w_mlp1': w((L, D, D_FF), D),
        'w_mlp2': w((L, D_FF, D), D_FF),
    }


def reference(x, c, ctx, c_ctx, w_mod, b_mod, norm1_w, norm2_w, w_in, conv_w, da_q_norm, da_k_norm,
              da_lambda, da_subln, mla_q_a_norm, w_q_b, mla_kv_a_norm, w_kv_b, mla_q_norm, mla_k_norm,
              w_conv_out, w_da_out, w_mla_out, w_o, w_mlp1, w_mlp2):
    S = x.shape[1]
    ROWS = S // GRID_W
    rows = jnp.repeat(jnp.arange(ROWS, dtype=jnp.int32), GRID_W)
    cols = jnp.tile(jnp.arange(GRID_W, dtype=jnp.int32), ROWS)
    pos = (rows, cols)

    silu_c = jax.nn.silu(c)
    silu_cc = jax.nn.silu(c_ctx)
    h, hc = x, ctx
    for l in range(DEPTH):
        last = l == DEPTH - 1
        lp = {
            'conv_w': conv_w[l], 'da_q_norm': da_q_norm[l], 'da_k_norm': da_k_norm[l],
            'da_subln': da_subln[l], 'mla_q_a_norm': mla_q_a_norm[l], 'w_q_b': w_q_b[l],
            'mla_kv_a_norm': mla_kv_a_norm[l], 'w_kv_b': w_kv_b[l], 'mla_q_norm': mla_q_norm[l],
            'mla_k_norm': mla_k_norm[l], 'w_conv_out': w_conv_out[l], 'w_da_out': w_da_out[l],
            'w_mla_out': w_mla_out[l], 'w_o': w_o[l],
        }
        lam_init = 0.8 - 0.6 * math.exp(-0.3 * l)
        lamp = da_lambda[l].astype(jnp.float32)
        lam = jnp.exp(jnp.sum(lamp[0] * lamp[1])) - jnp.exp(jnp.sum(lamp[2] * lamp[3])) + lam_init

        mod = (jnp.einsum('bd,de->be', silu_c, w_mod[l]) + b_mod[l])[:, None, :]
        modc = jnp.einsum('d,de->e', silu_cc, w_mod[l]) + b_mod[l]
        sh1, sc1, g1, sh2, sc2, g2 = jnp.split(mod, 6, axis=-1)
        csh1, csc1, cg1, csh2, csc2, cg2 = jnp.split(modc, 6, axis=-1)

        xn = _modulate(_rmsnorm(h, norm1_w[l]), sh1, sc1)
        xnc = _modulate(_rmsnorm(hc, norm1_w[l]), csh1, csc1)
        parts = _split(jnp.einsum('bsd,dc->bsc', xn, w_in[l]), Q_SIZES + KV_SIZES)
        q_parts, kv_parts = parts[:6], parts[6:]
        if last:
            kv_parts_c = _split(jnp.einsum('bsd,dc->bsc', xnc, w_in[l][:, Q_COLS:]), KV_SIZES)
        else:
            parts_c = _split(jnp.einsum('bsd,dc->bsc', xnc, w_in[l]), Q_SIZES + KV_SIZES)
            q_parts_c, kv_parts_c = parts_c[:6], parts_c[6:]
        keys_c = _stream_keys(kv_parts_c, lp, None)
        keys_l = _stream_keys(kv_parts, lp, pos)
        keys_all = tuple(jnp.concatenate([kc, kl], axis=1) for kc, kl in zip(keys_c, keys_l))

        h = h + g1 * _mixer(q_parts, keys_all, lp, lam, lam_init, pos, _sweep_blocks)
        if not last:
            hc = hc + cg1 * _mixer(q_parts_c, keys_c, lp, lam, lam_init, None, _direct)

        h = h + g2 * _ffn(_modulate(_rmsnorm(h, norm2_w[l]), sh2, sc2), w_mlp1[l], w_mlp2[l])
        if not last:
            hc = hc + cg2 * _ffn(_modulate(_rmsnorm(hc, norm2_w[l]), csh2, csc2), w_mlp1[l], w_mlp2[l])
    return h
```

```python
import functools
import math

import jax
import jax.numpy as jnp
from jax import lax
from jax.experimental import pallas as pl
from jax.experimental.pallas import tpu as pltpu

F32 = jnp.float32
BF16 = jnp.bfloat16

GRID_W = 64
ROPE_THETA = 10000.0
NORM_EPS = 1e-6
CONV_W = 1024
CONV_K = 3
DA_HEADS = 8
DA_HEAD = 64
DA_V = 2 * DA_HEAD
DA_SCALE = DA_HEAD ** -0.5
MLA_HEADS = 8
MLA_NOPE = 128
MLA_ROPE = 64
MLA_V = 128
MLA_QK = MLA_NOPE + MLA_ROPE
MLA_QK_PAD = 256
Q_LORA = 512
KV_LORA = 512
MLA_SCALE = MLA_QK ** -0.5
N_BRANCH = 3

LANE = 128
SUBLANE = 8

COL_CONV_B = 0
COL_CONV_C = CONV_W
COL_CONV_H = 2 * CONV_W
COL_DA_Q = 3 * CONV_W
COL_DA_K = COL_DA_Q + DA_HEADS * 2 * DA_HEAD
COL_DA_V = COL_DA_K + DA_HEADS * 2 * DA_HEAD
COL_CQ = COL_DA_V + DA_HEADS * DA_V
COL_CKV = COL_CQ + Q_LORA
COL_KROPE = COL_CKV + KV_LORA
COL_GATES = 8192

TM = 512
TM_OPROJ = 256
T_CONV = 256
TQ_DA = 256
TQ_MLA = 512
TK = 512
VMEM_LIMIT = 52 * 1024 * 1024


def _cparams(n_axes, vmem=VMEM_LIMIT):
    return pltpu.CompilerParams(dimension_semantics=("arbitrary",) * n_axes, vmem_limit_bytes=vmem)


def _norm_mod(h, g, sh, sc):
    ms = jnp.mean(h * h, axis=-1, keepdims=True)
    return (h * lax.rsqrt(ms + NORM_EPS) * g) * (1.0 + sc) + sh


def _group_ms(x, m_ref):
    return jnp.dot((x * x).astype(BF16), m_ref[...], preferred_element_type=F32)


def _rope(r, cos, sa, sb):
    return r * cos + pltpu.roll(r, LANE - 16, 1) * sa + pltpu.roll(r, 16, 1) * sb


def _mod_kernel(c_ref, w_ref, b_ref, o_ref):
    c = c_ref[...]
    s = c * (1.0 / (1.0 + jnp.exp(-c)))
    o_ref[...] = jnp.dot(s.astype(BF16), w_ref[...].astype(BF16), preferred_element_type=F32) + b_ref[...]


def _mod_call(cin, w_mod, b_mod):
    L, D, N = w_mod.shape
    tn = 1024
    return pl.pallas_call(
        _mod_kernel,
        grid=(L, N // tn),
        in_specs=[pl.BlockSpec((16, D), lambda l, j: (0, 0)),
                  pl.BlockSpec((None, D, tn), lambda l, j: (l, 0, j)),
                  pl.BlockSpec((None, 1, tn), lambda l, j: (l, 0, j))],
        out_specs=pl.BlockSpec((None, 16, tn), lambda l, j: (l, 0, j)),
        out_shape=jax.ShapeDtypeStruct((L, 16, N), F32),
        compiler_params=_cparams(2),
        name="adaln_mod",
    )(cin, w_mod, b_mod.reshape(L, 1, N))


def _mod_spec(l, k, D, tm, S, B):
    return pl.BlockSpec((None, None, 1, D), lambda i, *_: (l, jnp.minimum(i * tm // S, B), 0, k))


def _norm_mod_kernel(h_ref, g_ref, sh_ref, sc_ref, o_ref):
    o_ref[...] = _norm_mod(h_ref[...], g_ref[...], sh_ref[...], sc_ref[...]).astype(BF16)


def _norm_mod_call(h, g, modv, l, S, B):
    NT, D = h.shape
    return pl.pallas_call(
        _norm_mod_kernel,
        grid=(NT // TM,),
        in_specs=[pl.BlockSpec((TM, D), lambda i: (i, 0)),
                  pl.BlockSpec((1, D), lambda i: (0, 0)),
                  _mod_spec(l, 0, D, TM, S, B),
                  _mod_spec(l, 1, D, TM, S, B)],
        out_specs=pl.BlockSpec((TM, D), lambda i: (i, 0)),
        out_shape=jax.ShapeDtypeStruct((NT, D), BF16),
        compiler_params=_cparams(1),
        name="norm_mod",
    )(h, g, modv, modv)


def _proj_kernel(x_ref, w_ref, o_ref):
    o_ref[...] = jnp.dot(x_ref[...], w_ref[...], preferred_element_type=F32).astype(BF16)


def _proj_call(xn, w):
    NT, D = xn.shape
    NP = w.shape[1]
    tn = 2048 if NP % 2048 == 0 else 1024
    return pl.pallas_call(
        _proj_kernel,
        grid=(NP // tn, NT // TM),
        in_specs=[pl.BlockSpec((TM, D), lambda j, i: (i, 0)),
                  pl.BlockSpec((D, tn), lambda j, i: (0, j))],
        out_specs=pl.BlockSpec((TM, tn), lambda j, i: (i, j)),
        out_shape=jax.ShapeDtypeStruct((NT, NP), BF16),
        compiler_params=_cparams(2),
        name="in_proj",
    )(xn, w)


def _conv_kernel(b_ref, c_ref, h_ref, cp_ref, hp_ref, cn_ref, hn_ref, w_ref, o_ref, *, starts, ends):
    i = pl.program_id(0)
    u = c_ref[...].astype(F32) * h_ref[...].astype(F32)
    up = cp_ref[SUBLANE - 1:SUBLANE, :].astype(F32) * hp_ref[SUBLANE - 1:SUBLANE, :].astype(F32)
    un = cn_ref[0:1, :].astype(F32) * hn_ref[0:1, :].astype(F32)
    is_start = functools.reduce(jnp.logical_or, [i == s for s in starts])
    is_end = functools.reduce(jnp.logical_or, [i == e for e in ends])
    up = jnp.where(is_start, 0.0, up)
    un = jnp.where(is_end, 0.0, un)
    t = u.shape[0]
    row = lax.broadcasted_iota(jnp.int32, u.shape, 0)
    u_dn = jnp.where(row == 0, up, pltpu.roll(u, 1, 0))
    u_up = jnp.where(row == t - 1, un, pltpu.roll(u, t - 1, 0))
    w = w_ref[...]
    y = w[0:1, :] * u_dn + w[1:2, :] * u + w[2:3, :] * u_up
    o_ref[...] = (b_ref[...].astype(F32) * y).astype(BF16)


def _conv_call(P, conv_w, n_rows, S, B, CTX):
    NT = P.shape[0]
    t = T_CONV
    r8 = t // SUBLANE
    cb = CONV_W // CONV_W
    starts = [b * S // t for b in range(B)] + [(B * S + b * CTX) // t for b in range(B)]
    ends = [(b + 1) * S // t - 1 for b in range(B)] + [(B * S + (b + 1) * CTX) // t - 1 for b in range(B)]
    last8 = NT // SUBLANE - 1

    def main(col):
        return pl.BlockSpec((t, CONV_W), lambda i: (i, col // CONV_W))

    def prev(col):
        return pl.BlockSpec((SUBLANE, CONV_W), lambda i: (jnp.maximum(i * r8 - 1, 0), col // CONV_W))

    def nxt(col):
        return pl.BlockSpec((SUBLANE, CONV_W), lambda i: (jnp.minimum((i + 1) * r8, last8), col // CONV_W))

    del cb
    return pl.pallas_call(
        functools.partial(_conv_kernel, starts=starts, ends=ends),
        grid=(n_rows // t,),
        in_specs=[main(COL_CONV_B), main(COL_CONV_C), main(COL_CONV_H),
                  prev(COL_CONV_C), prev(COL_CONV_H), nxt(COL_CONV_C), nxt(COL_CONV_H),
                  pl.BlockSpec((CONV_K, CONV_W), lambda i: (0, 0))],
        out_specs=pl.BlockSpec((t, CONV_W), lambda i: (i, 0)),
        out_shape=jax.ShapeDtypeStruct((n_rows, CONV_W), BF16),
        compiler_params=_cparams(1),
        name="short_conv",
    )(P, P, P, P, P, P, P, conv_w)


def _da_prep_kernel(q_ref, k_ref, cos_ref, sa_ref, sb_ref, qg_ref, kg_ref, m_ref, qo_ref, ko_ref):
    cos, sa, sb = cos_ref[...], sa_ref[...], sb_ref[...]
    for hd in range(DA_HEADS):
        sl = slice(hd * LANE, (hd + 1) * LANE)
        for src, g_ref, dst, scale in ((q_ref, qg_ref, qo_ref, DA_SCALE), (k_ref, kg_ref, ko_ref, 1.0)):
            x = src[:, sl].astype(F32)
            xn = x * lax.rsqrt(_group_ms(x, m_ref) + NORM_EPS) * g_ref[...]
            dst[:, sl] = (_rope(xn, cos, sa, sb) * scale).astype(BF16)


def _da_prep_call(P, tabs, qg, kg, m64x2):
    NT = P.shape[0]
    W = DA_HEADS * 2 * DA_HEAD
    tab = pl.BlockSpec((TM, LANE), lambda i: (i, 0))
    vec = pl.BlockSpec((1, LANE), lambda i: (0, 0))
    return pl.pallas_call(
        _da_prep_kernel,
        grid=(NT // TM,),
        in_specs=[pl.BlockSpec((TM, W), lambda i: (i, COL_DA_Q // W)),
                  pl.BlockSpec((TM, W), lambda i: (i, COL_DA_K // W)),
                  tab, tab, tab, vec, vec,
                  pl.BlockSpec((LANE, LANE), lambda i: (0, 0))],
        out_specs=[pl.BlockSpec((TM, W), lambda i: (i, 0)), pl.BlockSpec((TM, W), lambda i: (i, 0))],
        out_shape=[jax.ShapeDtypeStruct((NT, W), BF16), jax.ShapeDtypeStruct((NT, W), BF16)],
        compiler_params=_cparams(1),
        name="da_prep",
    )(P, P, *tabs, qg, kg, m64x2)


def _mla_q_kernel(cq_ref, an_ref, w_ref, g_ref, m_ref, cos_ref, sa_ref, sb_ref, o_ref):
    cos, sa, sb = cos_ref[...], sa_ref[...], sb_ref[...]
    x = cq_ref[...].astype(F32)
    xn = x * lax.rsqrt(jnp.mean(x * x, axis=-1, keepdims=True) + NORM_EPS) * an_ref[...]
    q = jnp.dot(xn.astype(BF16), w_ref[...], preferred_element_type=F32)
    for hd in range(MLA_HEADS):
        c0 = hd * MLA_QK_PAD
        qh = q[:, c0:c0 + MLA_QK_PAD]
        qn = qh * lax.rsqrt(_group_ms(qh, m_ref) + NORM_EPS) * g_ref[...]
        o_ref[:, c0:c0 + LANE] = (qn[:, :LANE] * MLA_SCALE).astype(BF16)
        o_ref[:, c0 + LANE:c0 + 2 * LANE] = (_rope(qn[:, LANE:], cos, sa, sb) * MLA_SCALE).astype(BF16)


def _mla_q_call(P, an, w, g, m256, tabs):
    NT = P.shape[0]
    W = MLA_HEADS * MLA_QK_PAD
    tab = pl.BlockSpec((TM, LANE), lambda i: (i, 0))
    return pl.pallas_call(
        _mla_q_kernel,
        grid=(NT // TM,),
        in_specs=[pl.BlockSpec((TM, Q_LORA), lambda i: (i, COL_CQ // Q_LORA)),
                  pl.BlockSpec((1, Q_LORA), lambda i: (0, 0)),
                  pl.BlockSpec((Q_LORA, W), lambda i: (0, 0)),
                  pl.BlockSpec((1, MLA_QK_PAD), lambda i: (0, 0)),
                  pl.BlockSpec((MLA_QK_PAD, MLA_QK_PAD), lambda i: (0, 0)),
                  tab, tab, tab],
        out_specs=pl.BlockSpec((TM, W), lambda i: (i, 0)),
        out_shape=jax.ShapeDtypeStruct((NT, W), BF16),
        compiler_params=_cparams(1),
        name="mla_q_prep",
    )(P, an, w, g, m256, *tabs)


def _mla_k_kernel(ckv_ref, kr_ref, an_ref, w_ref, gn_ref, gr_ref, m128_ref, m64_ref,
                  cos_ref, sa_ref, sb_ref, ko_ref, vo_ref):
    x = ckv_ref[...].astype(F32)
    xn = x * lax.rsqrt(jnp.mean(x * x, axis=-1, keepdims=True) + NORM_EPS) * an_ref[...]
    kv = jnp.dot(xn.astype(BF16), w_ref[...], preferred_element_type=F32)
    kr = kr_ref[...].astype(F32)
    krn = kr * lax.rsqrt(_group_ms(kr, m64_ref) + NORM_EPS) * gr_ref[...]
    krr = _rope(krn, cos_ref[...], sa_ref[...], sb_ref[...]).astype(BF16)
    for hd in range(MLA_HEADS):
        c0 = hd * (MLA_NOPE + MLA_V)
        kn = kv[:, c0:c0 + MLA_NOPE]
        knn = kn * lax.rsqrt(_group_ms(kn, m128_ref) + NORM_EPS) * gn_ref[...]
        ko_ref[:, hd * MLA_QK_PAD:hd * MLA_QK_PAD + LANE] = knn.astype(BF16)
        ko_ref[:, hd * MLA_QK_PAD + LANE:(hd + 1) * MLA_QK_PAD] = krr
        vo_ref[:, hd * MLA_V:(hd + 1) * MLA_V] = kv[:, c0 + MLA_NOPE:c0 + MLA_NOPE + MLA_V].astype(BF16)


def _mla_k_call(P, an, w, gn, gr, m128, m64, tabs):
    NT = P.shape[0]
    WK = MLA_HEADS * MLA_QK_PAD
    WV = MLA_HEADS * MLA_V
    tab = pl.BlockSpec((TM, LANE), lambda i: (i, 0))
    vec = pl.BlockSpec((1, LANE), lambda i: (0, 0))
    mat = pl.BlockSpec((LANE, LANE), lambda i: (0, 0))
    return pl.pallas_call(
        _mla_k_kernel,
        grid=(NT // TM,),
        in_specs=[pl.BlockSpec((TM, KV_LORA), lambda i: (i, COL_CKV // KV_LORA)),
                  pl.BlockSpec((TM, LANE), lambda i: (i, COL_KROPE // LANE)),
                  pl.BlockSpec((1, KV_LORA), lambda i: (0, 0)),
                  pl.BlockSpec((KV_LORA, MLA_HEADS * (MLA_NOPE + MLA_V)), lambda i: (0, 0)),
                  vec, vec, mat, mat, tab, tab, tab],
        out_specs=[pl.BlockSpec((TM, WK), lambda i: (i, 0)), pl.BlockSpec((TM, WV), lambda i: (i, 0))],
        out_shape=[jax.ShapeDtypeStruct((NT, WK), BF16), jax.ShapeDtypeStruct((NT, WV), BF16)],
        compiler_params=_cparams(1),
        name="mla_k_prep",
    )(P, P, an, w, gn, gr, m128, m64, *tabs)


def _attend(qz, kc_ref, vc_ref, kl_ref, vl_ref, n_lat):
    dims = (((1,), (1,)), ((), ()))

    s = lax.dot_general(qz, kc_ref[...], dims, preferred_element_type=F32)
    m = jnp.max(s, axis=1, keepdims=True)
    p = jnp.exp(s - m)
    l = jnp.sum(p, axis=1, keepdims=True)
    acc = jnp.dot(p.astype(BF16), vc_ref[...], preferred_element_type=F32)
    if n_lat == 0:
        return l, acc

    def body(j, carry):
        m, l, acc = carry
        off = pl.multiple_of(j * TK, TK)
        s = lax.dot_general(qz, kl_ref[pl.ds(off, TK), :], dims, preferred_element_type=F32)
        m_new = jnp.maximum(m, jnp.max(s, axis=1, keepdims=True))
        alpha = jnp.exp(m - m_new)
        p = jnp.exp(s - m_new)
        l = alpha * l + jnp.sum(p, axis=1, keepdims=True)
        acc = alpha * acc + jnp.dot(p.astype(BF16), vl_ref[pl.ds(off, TK), :], preferred_element_type=F32)
        return m_new, l, acc

    _, l, acc = lax.fori_loop(0, n_lat, body, (m, l, acc))
    return l, acc


def _da_attn_kernel(*refs, n_lat, lam_init):
    if n_lat:
        q_ref, kc_ref, vc_ref, kl_ref, vl_ref, lam_ref, sub_ref, o_ref = refs
    else:
        q_ref, kc_ref, vc_ref, lam_ref, sub_ref, _, o_ref = refs
        kl_ref = vl_ref = None
    q = q_ref[...]
    tq = q.shape[0]
    lane = lax.broadcasted_iota(jnp.int32, q.shape, 1)
    zero = jnp.zeros_like(q)
    qz = jnp.concatenate([jnp.where(lane < DA_HEAD, q, zero), jnp.where(lane >= DA_HEAD, q, zero)], axis=0)
    l, acc = _attend(qz, kc_ref, vc_ref, kl_ref, vl_ref, n_lat)
    o = acc / l
    lamp = lam_ref[...]
    lam = (jnp.exp(jnp.sum(lamp[0:1, :] * lamp[1:2, :], axis=1, keepdims=True))
           - jnp.exp(jnp.sum(lamp[2:3, :] * lamp[3:4, :], axis=1, keepdims=True)) + lam_init)
    d = o[:tq, :] - lam * o[tq:, :]
    ms = jnp.mean(d * d, axis=-1, keepdims=True)
    o_ref[...] = (d * lax.rsqrt(ms + NORM_EPS) * sub_ref[...] * (1.0 - lam_init)).astype(BF16)


def _mla_attn_kernel(*refs, n_lat):
    if n_lat:
        q_ref, kc_ref, vc_ref, kl_ref, vl_ref, o_ref = refs
    else:
        q_ref, kc_ref, vc_ref, _, o_ref = refs
        kl_ref = vl_ref = None
    l, acc = _attend(q_ref[...], kc_ref, vc_ref, kl_ref, vl_ref, n_lat)
    o_ref[...] = (acc / l).astype(BF16)


def _attn_calls(kernel_fn, name, q_arr, k_arr, v_arr, extra, dq, dv, k_col0, v_col0, tq, S, B, CTX, heads, with_ctx):
    NT = q_arr.shape[0]
    ctx_blk0 = (B * S) // CTX
    extra_specs = [pl.BlockSpec(e.shape, lambda b, h, i: (0, 0)) for e in extra]
    nq = S // tq
    out_shape = jax.ShapeDtypeStruct((NT, heads * dv), BF16)
    kc = pl.BlockSpec((CTX, dq), lambda b, h, i: (ctx_blk0 + b, k_col0 // dq + h))
    vc = pl.BlockSpec((CTX, dv), lambda b, h, i: (ctx_blk0 + b, v_col0 // dv + h))
    out = pl.pallas_call(
        functools.partial(kernel_fn, n_lat=S // TK),
        grid=(B, heads, nq),
        in_specs=[pl.BlockSpec((tq, dq), lambda b, h, i: (b * nq + i, h)), kc, vc,
                  pl.BlockSpec((S, dq), lambda b, h, i: (b, k_col0 // dq + h)),
                  pl.BlockSpec((S, dv), lambda b, h, i: (b, v_col0 // dv + h))] + extra_specs,
        out_specs=pl.BlockSpec((tq, dv), lambda b, h, i: (b * nq + i, h)),
        out_shape=out_shape,
        compiler_params=_cparams(3),
        name=name,
    )(q_arr, k_arr, v_arr, k_arr, v_arr, *extra)
    if not with_ctx:
        return out
    n_in = 3 + len(extra)
    return pl.pallas_call(
        functools.partial(kernel_fn, n_lat=0),
        grid=(B, heads, 1),
        in_specs=[pl.BlockSpec((CTX, dq), lambda b, h, i: (ctx_blk0 + b, h)), kc, vc] + extra_specs
                 + [pl.BlockSpec(memory_space=pl.ANY)],
        out_specs=pl.BlockSpec((CTX, dv), lambda b, h, i: (ctx_blk0 + b, h)),
        out_shape=out_shape,
        input_output_aliases={n_in: 0},
        compiler_params=_cparams(3),
        name=name + "_ctx",
    )(q_arr, k_arr, v_arr, *extra, out)


def _merge_kernel(yc_ref, od_ref, om_ref, gc_ref, gd_ref, gm_ref, wc_ref, wd_ref, wm_ref, o_ref):
    def branch(a_ref, g_ref, w_ref):
        g = g_ref[...].astype(F32)
        return (1.0 / (1.0 + jnp.exp(-g))) * jnp.dot(a_ref[...], w_ref[...], preferred_element_type=F32)

    o_ref[...] = (branch(yc_ref, gc_ref, wc_ref) + branch(od_ref, gd_ref, wd_ref)
                  + branch(om_ref, gm_ref, wm_ref)).astype(BF16)


def _merge_call(yc, od, om, P, wc, wd, wm, n_rows, D):
    tn = 1024
    act = pl.BlockSpec((TM, CONV_W), lambda j, i: (i, 0))

    def gate(k):
        return pl.BlockSpec((TM, tn), lambda j, i: (i, (COL_GATES + k * D) // tn + j))

    wsp = pl.BlockSpec((CONV_W, tn), lambda j, i: (0, j))
    return pl.pallas_call(
        _merge_kernel,
        grid=(D // tn, n_rows // TM),
        in_specs=[act, act, act, gate(0), gate(1), gate(2), wsp, wsp, wsp],
        out_specs=pl.BlockSpec((TM, tn), lambda j, i: (i, j)),
        out_shape=jax.ShapeDtypeStruct((n_rows, D), BF16),
        compiler_params=_cparams(2),
        name="branch_merge",
    )(yc, od, om, P, P, P, wc, wd, wm)


def _oproj_kernel(y_ref, w_ref, h_ref, g1_ref, n2_ref, sh2_ref, sc2_ref, ho_ref, xo_ref):
    h1 = h_ref[...] + g1_ref[...] * jnp.dot(y_ref[...], w_ref[...], preferred_element_type=F32)
    ho_ref[...] = h1
    xo_ref[...] = _norm_mod(h1, n2_ref[...], sh2_ref[...], sc2_ref[...]).astype(BF16)


def _oproj_call(y, w_o, h, n2, modv, l, n_rows, S, B):
    D = w_o.shape[0]
    tm = TM_OPROJ
    row = pl.BlockSpec((tm, D), lambda i: (i, 0))
    return pl.pallas_call(
        _oproj_kernel,
        grid=(n_rows // tm,),
        in_specs=[row, pl.BlockSpec((D, D), lambda i: (0, 0)), row,
                  _mod_spec(l, 2, D, tm, S, B),
                  pl.BlockSpec((1, D), lambda i: (0, 0)),
                  _mod_spec(l, 3, D, tm, S, B), _mod_spec(l, 4, D, tm, S, B)],
        out_specs=[row, row],
        out_shape=[jax.ShapeDtypeStruct((n_rows, D), F32), jax.ShapeDtypeStruct((n_rows, D), BF16)],
        compiler_params=_cparams(1),
        name="out_proj",
    )(y, w_o, h, modv, n2, modv, modv)


def _ffn_kernel(*refs, emit_xn):
    if emit_xn:
        x_ref, w1_ref, w2_ref, h_ref, g2_ref, n_ref, sh_ref, sc_ref, ho_ref, xo_ref, acc_ref = refs
    else:
        x_ref, w1_ref, w2_ref, h_ref, g2_ref, ho_ref, acc_ref = refs
    f = pl.program_id(1)

    @pl.when(f == 0)
    def _():
        acc_ref[...] = jnp.zeros_like(acc_ref)

    hid = jnp.maximum(jnp.dot(x_ref[...], w1_ref[...], preferred_element_type=F32), 0.0)
    acc_ref[...] += jnp.dot((hid * hid).astype(BF16), w2_ref[...], preferred_element_type=F32)

    @pl.when(f == pl.num_programs(1) - 1)
    def _():
        h2 = h_ref[...] + g2_ref[...] * acc_ref[...]
        ho_ref[...] = h2
        if emit_xn:
            xo_ref[...] = _norm_mod(h2, n_ref[...], sh_ref[...], sc_ref[...]).astype(BF16)


def _ffn_call(xn, w1, w2, h, modv, l, n_rows, S, B, n_next):
    D, FF = w1.shape
    tf = 512
    emit_xn = n_next is not None
    row = pl.BlockSpec((TM, D), lambda i, f: (i, 0))
    in_specs = [row, pl.BlockSpec((D, tf), lambda i, f: (0, f)), pl.BlockSpec((tf, D), lambda i, f: (f, 0)),
                row, _mod_spec(l, 5, D, TM, S, B)]
    args = [xn, w1, w2, h, modv]
    out_specs = [row]
    out_shape = [jax.ShapeDtypeStruct((n_rows, D), F32)]
    if emit_xn:
        in_specs += [pl.BlockSpec((1, D), lambda i, f: (0, 0)),
                     _mod_spec(l + 1, 0, D, TM, S, B), _mod_spec(l + 1, 1, D, TM, S, B)]
        args += [n_next, modv, modv]
        out_specs.append(row)
        out_shape.append(jax.ShapeDtypeStruct((n_rows, D), BF16))
    res = pl.pallas_call(
        functools.partial(_ffn_kernel, emit_xn=emit_xn),
        grid=(n_rows // TM, FF // tf),
        in_specs=in_specs,
        out_specs=out_specs,
        out_shape=out_shape,
        scratch_shapes=[pltpu.VMEM((TM, D), F32)],
        compiler_params=_cparams(2),
        name="ffn",
    )(*args)
    return (res[0], res[1]) if emit_xn else (res[0], None)


def _rope_tables(B, S, CTX):
    half = DA_HEAD // 2
    inv_freq = ROPE_THETA ** (-jnp.arange(0, half, 2, dtype=F32) / half)
    s_idx = jnp.arange(S, dtype=jnp.int32)
    rows = (s_idx // GRID_W).astype(F32)
    cols = (s_idx % GRID_W).astype(F32)
    ang_r = rows[:, None] * inv_freq[None, :]
    ang_c = cols[:, None] * inv_freq[None, :]
    ang = jnp.concatenate([ang_r, ang_r, ang_c, ang_c], axis=1)
    cos, sin = jnp.cos(ang), jnp.sin(ang)
    first = (jnp.arange(DA_HEAD) % half) < (half // 2)
    sa = jnp.where(first[None, :], -sin, 0.0)
    sb = jnp.where(first[None, :], 0.0, sin)

    def full(t, ctx_val):
        t = jnp.tile(jnp.concatenate([t, t], axis=1), (B, 1))
        return jnp.concatenate([t, jnp.full((B * CTX, LANE), ctx_val, F32)], axis=0)

    return full(cos, 1.0), full(sa, 0.0), full(sb, 0.0)


def _avg_matrix(groups):
    n = LANE * ((max(s + w for s, w in groups) + LANE - 1) // LANE)
    idx = jnp.arange(n)
    m = jnp.zeros((n, n), F32)
    for s, w in groups:
        inside = (idx >= s) & (idx < s + w)
        m = m + jnp.where(inside[:, None] & inside[None, :], 1.0 / w, 0.0)
    return m.astype(BF16)


def kernel(x, c, ctx, c_ctx, w_mod, b_mod, norm1_w, norm2_w, w_in, conv_w, da_q_norm, da_k_norm, da_lambda,
           da_subln, mla_q_a_norm, w_q_b, mla_kv_a_norm, w_kv_b, mla_q_norm, mla_k_norm, w_conv_out, w_da_out,
           w_mla_out, w_o, w_mlp1, w_mlp2):
    B, S, D = x.shape
    CTX = ctx.shape[1]
    L = w_mod.shape[0]
    NL = B * S
    NT = NL + B * CTX
    assert S % TM == 0 and S % TK == 0 and S % GRID_W == 0 and (B * CTX) % TM == 0 and CTX == T_CONV
    assert COL_KROPE + LANE <= COL_GATES and D % 1024 == 0 and B + 1 <= 16

    h = jnp.concatenate([x.reshape(NL, D), ctx.reshape(B * CTX, D)], axis=0)
    cin = jnp.concatenate([c, c_ctx[None, :], jnp.zeros((16 - B - 1, D), F32)], axis=0)
    modv = _mod_call(cin, w_mod, b_mod)[:, :B + 1].reshape(L, B + 1, 1, 6 * D)

    tabs = _rope_tables(B, S, CTX)
    m64x2 = _avg_matrix([(0, DA_HEAD), (DA_HEAD, DA_HEAD)])
    m256 = _avg_matrix([(0, MLA_NOPE), (MLA_NOPE, MLA_ROPE)])
    m128 = _avg_matrix([(0, MLA_NOPE)])
    m64 = _avg_matrix([(0, MLA_ROPE)])

    q_end = 3 * CONV_W + DA_HEADS * 2 * DA_HEAD
    cq_end = q_end + Q_LORA
    kv0 = cq_end + N_BRANCH * D
    dkv_end = kv0 + DA_HEADS * 2 * DA_HEAD + DA_HEADS * DA_V

    xn = _norm_mod_call(h, norm1_w[0][None, :], modv, 0, S, B)
    for l in range(L):
        last = l == L - 1
        n_rows = NL if last else NT
        lam_init = 0.8 - 0.6 * math.exp(-0.3 * l)

        w = w_in[l].astype(BF16)
        w_perm = jnp.concatenate(
            [w[:, :q_end], w[:, kv0:dkv_end], w[:, q_end:cq_end], w[:, dkv_end:],
             jnp.zeros((D, COL_GATES - COL_KROPE - MLA_ROPE), BF16), w[:, cq_end:kv0]], axis=1)
        wqb = jnp.pad(w_q_b[l].astype(BF16).reshape(Q_LORA, MLA_HEADS, MLA_QK),
                      ((0, 0), (0, 0), (0, MLA_QK_PAD - MLA_QK))).reshape(Q_LORA, MLA_HEADS * MLA_QK_PAD)
        zpad = jnp.zeros((MLA_QK_PAD - MLA_QK,), F32)
        qg = jnp.tile(da_q_norm[l], 2)[None, :]
        kg = jnp.tile(da_k_norm[l], 2)[None, :]
        mq_g = jnp.concatenate([mla_q_norm[l], zpad])[None, :]
        mk_gn = mla_k_norm[l][None, :MLA_NOPE]
        mk_gr = jnp.concatenate([mla_k_norm[l][MLA_NOPE:], zpad])[None, :]

        P = _proj_call(xn, w_perm)
        yc = _conv_call(P, conv_w[l], n_rows, S, B, CTX)
        q_da, k_da = _da_prep_call(P, tabs, qg, kg, m64x2)
        q_m = _mla_q_call(P, mla_q_a_norm[l][None, :], wqb, mq_g, m256, tabs)
        k_m, v_m = _mla_k_call(P, mla_kv_a_norm[l][None, :], w_kv_b[l].astype(BF16), mk_gn, mk_gr, m128, m64, tabs)

        o_da = _attn_calls(functools.partial(_da_attn_kernel, lam_init=lam_init), "da_attn",
                           q_da, k_da, P, [da_lambda[l], da_subln[l][None, :]],
                           2 * DA_HEAD, DA_V, 0, COL_DA_V, TQ_DA, S, B, CTX, DA_HEADS, not last)
        o_m = _attn_calls(_mla_attn_kernel, "mla_attn", q_m, k_m, v_m, [],
                          MLA_QK_PAD, MLA_V, 0, 0, TQ_MLA, S, B, CTX, MLA_HEADS, not last)

        y = _merge_call(yc, o_da, o_m, P, w_conv_out[l].astype(BF16), w_da_out[l].astype(BF16),
                        w_mla_out[l].astype(BF16), n_rows, D)
        h1, xn2 = _oproj_call(y, w_o[l].astype(BF16), h, norm2_w[l][None, :], modv, l, n_rows, S, B)
        h, xn = _ffn_call(xn2, w_mlp1[l].astype(BF16), w_mlp2[l].astype(BF16), h1, modv, l, n_rows, S, B,
                          None if last else norm1_w[l + 1][None, :])
    return h.reshape(B, S, D)
```

```python
import functools
import math

import jax
import jax.numpy as jnp
from jax import lax
from jax.experimental import pallas as pl
from jax.experimental.pallas import tpu as pltpu

F32 = jnp.float32
BF16 = jnp.bfloat16

GRID_W = 64
ROPE_THETA = 10000.0
NORM_EPS = 1e-6
CONV_W = 1024
CONV_K = 3
DA_HEADS = 8
DA_HEAD = 64
DA_V = 2 * DA_HEAD
DA_SCALE = DA_HEAD ** -0.5
MLA_HEADS = 8
MLA_NOPE = 128
MLA_ROPE = 64
MLA_V = 128
MLA_QK = MLA_NOPE + MLA_ROPE
MLA_QK_PAD = 256
Q_LORA = 512
KV_LORA = 512
MLA_SCALE = MLA_QK ** -0.5
N_BRANCH = 3
LOG2E = math.log2(math.e)
FAST_MAX_SHIFT = 50.0

LANE = 128
SUBLANE = 8

COL_CONV_B = 0
COL_CONV_C = CONV_W
COL_CONV_H = 2 * CONV_W
COL_DA_Q = 3 * CONV_W
COL_DA_K = COL_DA_Q + DA_HEADS * 2 * DA_HEAD
COL_DA_V = COL_DA_K + DA_HEADS * 2 * DA_HEAD
COL_CQ = COL_DA_V + DA_HEADS * DA_V
COL_CKV = COL_CQ + Q_LORA
COL_KROPE = COL_CKV + KV_LORA
COL_GATES = 8192

TM = 512
TM_OPROJ = 256
T_CONV = 256
TQ_DA = 256
TQ_MLA = 512
TK = 512
TK_FIXED = 2048
VMEM_LIMIT = 52 * 1024 * 1024


def _cparams(n_axes, vmem=VMEM_LIMIT):
    return pltpu.CompilerParams(dimension_semantics=("arbitrary",) * n_axes, vmem_limit_bytes=vmem)


def _norm_mod(h, g, sh, sc):
    ms = jnp.mean(h * h, axis=-1, keepdims=True)
    return (h * lax.rsqrt(ms + NORM_EPS) * g) * (1.0 + sc) + sh


def _group_ms(x, m_ref):
    return jnp.dot((x * x).astype(BF16), m_ref[...], preferred_element_type=F32)


def _rope(r, cos, sa, sb):
    return r * cos + pltpu.roll(r, LANE - 16, 1) * sa + pltpu.roll(r, 16, 1) * sb


def _mod_kernel(c_ref, w_ref, b_ref, o_ref):
    c = c_ref[...]
    s = c * (1.0 / (1.0 + jnp.exp(-c)))
    o_ref[...] = jnp.dot(s.astype(BF16), w_ref[...].astype(BF16), preferred_element_type=F32) + b_ref[...]


def _mod_call(cin, w_mod, b_mod):
    L, D, N = w_mod.shape
    tn = 1024
    return pl.pallas_call(
        _mod_kernel,
        grid=(L, N // tn),
        in_specs=[pl.BlockSpec((16, D), lambda l, j: (0, 0)),
                  pl.BlockSpec((None, D, tn), lambda l, j: (l, 0, j)),
                  pl.BlockSpec((None, 1, tn), lambda l, j: (l, 0, j))],
        out_specs=pl.BlockSpec((None, 16, tn), lambda l, j: (l, 0, j)),
        out_shape=jax.ShapeDtypeStruct((L, 16, N), F32),
        compiler_params=_cparams(2),
        name="adaln_mod",
    )(cin, w_mod, b_mod.reshape(L, 1, N))


def _mod_spec(l, k, D, tm, S, B):
    return pl.BlockSpec((None, None, 1, D), lambda i, *_: (l, jnp.minimum(i * tm // S, B), 0, k))


def _norm_mod_kernel(h_ref, g_ref, sh_ref, sc_ref, o_ref):
    o_ref[...] = _norm_mod(h_ref[...], g_ref[...], sh_ref[...], sc_ref[...]).astype(BF16)


def _norm_mod_call(h, g, modv, l, S, B):
    NT, D = h.shape
    return pl.pallas_call(
        _norm_mod_kernel,
        grid=(NT // TM,),
        in_specs=[pl.BlockSpec((TM, D), lambda i: (i, 0)),
                  pl.BlockSpec((1, D), lambda i: (0, 0)),
                  _mod_spec(l, 0, D, TM, S, B),
                  _mod_spec(l, 1, D, TM, S, B)],
        out_specs=pl.BlockSpec((TM, D), lambda i: (i, 0)),
        out_shape=jax.ShapeDtypeStruct((NT, D), BF16),
        compiler_params=_cparams(1),
        name="norm_mod",
    )(h, g, modv, modv)


def _proj_kernel(x_ref, w_ref, o_ref):
    o_ref[...] = jnp.dot(x_ref[...], w_ref[...], preferred_element_type=F32).astype(BF16)


def _proj_call(xn, w):
    NT, D = xn.shape
    NP = w.shape[1]
    tn = 2048 if NP % 2048 == 0 else 1024
    return pl.pallas_call(
        _proj_kernel,
        grid=(NP // tn, NT // TM),
        in_specs=[pl.BlockSpec((TM, D), lambda j, i: (i, 0)),
                  pl.BlockSpec((D, tn), lambda j, i: (0, j))],
        out_specs=pl.BlockSpec((TM, tn), lambda j, i: (i, j)),
        out_shape=jax.ShapeDtypeStruct((NT, NP), BF16),
        compiler_params=_cparams(2),
        name="in_proj",
    )(xn, w)


def _conv_kernel(b_ref, c_ref, h_ref, cp_ref, hp_ref, cn_ref, hn_ref, w_ref, o_ref, *, starts, ends):
    i = pl.program_id(0)
    u = c_ref[...].astype(F32) * h_ref[...].astype(F32)
    up = cp_ref[SUBLANE - 1:SUBLANE, :].astype(F32) * hp_ref[SUBLANE - 1:SUBLANE, :].astype(F32)
    un = cn_ref[0:1, :].astype(F32) * hn_ref[0:1, :].astype(F32)
    is_start = functools.reduce(jnp.logical_or, [i == s for s in starts])
    is_end = functools.reduce(jnp.logical_or, [i == e for e in ends])
    up = jnp.where(is_start, 0.0, up)
    un = jnp.where(is_end, 0.0, un)
    t = u.shape[0]
    row = lax.broadcasted_iota(jnp.int32, u.shape, 0)
    u_dn = jnp.where(row == 0, up, pltpu.roll(u, 1, 0))
    u_up = jnp.where(row == t - 1, un, pltpu.roll(u, t - 1, 0))
    w = w_ref[...]
    y = w[0:1, :] * u_dn + w[1:2, :] * u + w[2:3, :] * u_up
    o_ref[...] = (b_ref[...].astype(F32) * y).astype(BF16)


def _conv_call(P, conv_w, n_rows, S, B, CTX):
    NT = P.shape[0]
    t = T_CONV
    r8 = t // SUBLANE
    cb = CONV_W // CONV_W
    starts = [b * S // t for b in range(B)] + [(B * S + b * CTX) // t for b in range(B)]
    ends = [(b + 1) * S // t - 1 for b in range(B)] + [(B * S + (b + 1) * CTX) // t - 1 for b in range(B)]
    last8 = NT // SUBLANE - 1

    def main(col):
        return pl.BlockSpec((t, CONV_W), lambda i: (i, col // CONV_W))

    def prev(col):
        return pl.BlockSpec((SUBLANE, CONV_W), lambda i: (jnp.maximum(i * r8 - 1, 0), col // CONV_W))

    def nxt(col):
        return pl.BlockSpec((SUBLANE, CONV_W), lambda i: (jnp.minimum((i + 1) * r8, last8), col // CONV_W))

    del cb
    return pl.pallas_call(
        functools.partial(_conv_kernel, starts=starts, ends=ends),
        grid=(n_rows // t,),
        in_specs=[main(COL_CONV_B), main(COL_CONV_C), main(COL_CONV_H),
                  prev(COL_CONV_C), prev(COL_CONV_H), nxt(COL_CONV_C), nxt(COL_CONV_H),
                  pl.BlockSpec((CONV_K, CONV_W), lambda i: (0, 0))],
        out_specs=pl.BlockSpec((t, CONV_W), lambda i: (i, 0)),
        out_shape=jax.ShapeDtypeStruct((n_rows, CONV_W), BF16),
        compiler_params=_cparams(1),
        name="short_conv",
    )(P, P, P, P, P, P, P, conv_w)


def _da_prep_kernel(q_ref, k_ref, cos_ref, sa_ref, sb_ref, qg_ref, kg_ref, m_ref, qo_ref, ko_ref):
    cos, sa, sb = cos_ref[...], sa_ref[...], sb_ref[...]
    for hd in range(DA_HEADS):
        sl = slice(hd * LANE, (hd + 1) * LANE)
        for src, g_ref, dst, scale in ((q_ref, qg_ref, qo_ref, DA_SCALE * LOG2E), (k_ref, kg_ref, ko_ref, 1.0)):
            x = src[:, sl].astype(F32)
            xn = x * lax.rsqrt(_group_ms(x, m_ref) + NORM_EPS) * g_ref[...]
            dst[:, sl] = (_rope(xn, cos, sa, sb) * scale).astype(BF16)


def _da_prep_call(P, tabs, qg, kg, m64x2):
    NT = P.shape[0]
    W = DA_HEADS * 2 * DA_HEAD
    tab = pl.BlockSpec((TM, LANE), lambda i: (i, 0))
    vec = pl.BlockSpec((1, LANE), lambda i: (0, 0))
    return pl.pallas_call(
        _da_prep_kernel,
        grid=(NT // TM,),
        in_specs=[pl.BlockSpec((TM, W), lambda i: (i, COL_DA_Q // W)),
                  pl.BlockSpec((TM, W), lambda i: (i, COL_DA_K // W)),
                  tab, tab, tab, vec, vec,
                  pl.BlockSpec((LANE, LANE), lambda i: (0, 0))],
        out_specs=[pl.BlockSpec((TM, W), lambda i: (i, 0)), pl.BlockSpec((TM, W), lambda i: (i, 0))],
        out_shape=[jax.ShapeDtypeStruct((NT, W), BF16), jax.ShapeDtypeStruct((NT, W), BF16)],
        compiler_params=_cparams(1),
        name="da_prep",
    )(P, P, *tabs, qg, kg, m64x2)


def _mla_q_kernel(cq_ref, an_ref, w_ref, g_ref, m_ref, cos_ref, sa_ref, sb_ref, o_ref):
    cos, sa, sb = cos_ref[...], sa_ref[...], sb_ref[...]
    x = cq_ref[...].astype(F32)
    xn = x * lax.rsqrt(jnp.mean(x * x, axis=-1, keepdims=True) + NORM_EPS) * an_ref[...]
    q = jnp.dot(xn.astype(BF16), w_ref[...], preferred_element_type=F32)
    for hd in range(MLA_HEADS):
        c0 = hd * MLA_QK_PAD
        qh = q[:, c0:c0 + MLA_QK_PAD]
        qn = qh * lax.rsqrt(_group_ms(qh, m_ref) + NORM_EPS) * g_ref[...]
        o_ref[:, c0:c0 + LANE] = (qn[:, :LANE] * (MLA_SCALE * LOG2E)).astype(BF16)
        o_ref[:, c0 + LANE:c0 + 2 * LANE] = (_rope(qn[:, LANE:], cos, sa, sb) * (MLA_SCALE * LOG2E)).astype(BF16)


def _mla_q_call(P, an, w, g, m256, tabs):
    NT = P.shape[0]
    W = MLA_HEADS * MLA_QK_PAD
    tab = pl.BlockSpec((TM, LANE), lambda i: (i, 0))
    return pl.pallas_call(
        _mla_q_kernel,
        grid=(NT // TM,),
        in_specs=[pl.BlockSpec((TM, Q_LORA), lambda i: (i, COL_CQ // Q_LORA)),
                  pl.BlockSpec((1, Q_LORA), lambda i: (0, 0)),
                  pl.BlockSpec((Q_LORA, W), lambda i: (0, 0)),
                  pl.BlockSpec((1, MLA_QK_PAD), lambda i: (0, 0)),
                  pl.BlockSpec((MLA_QK_PAD, MLA_QK_PAD), lambda i: (0, 0)),
                  tab, tab, tab],
        out_specs=pl.BlockSpec((TM, W), lambda i: (i, 0)),
        out_shape=jax.ShapeDtypeStruct((NT, W), BF16),
        compiler_params=_cparams(1),
        name="mla_q_prep",
    )(P, an, w, g, m256, *tabs)


def _mla_k_kernel(ckv_ref, kr_ref, an_ref, w_ref, gn_ref, gr_ref, m128_ref, m64_ref,
                  cos_ref, sa_ref, sb_ref, ko_ref, vo_ref):
    x = ckv_ref[...].astype(F32)
    xn = x * lax.rsqrt(jnp.mean(x * x, axis=-1, keepdims=True) + NORM_EPS) * an_ref[...]
    kv = jnp.dot(xn.astype(BF16), w_ref[...], preferred_element_type=F32)
    kr = kr_ref[...].astype(F32)
    krn = kr * lax.rsqrt(_group_ms(kr, m64_ref) + NORM_EPS) * gr_ref[...]
    krr = _rope(krn, cos_ref[...], sa_ref[...], sb_ref[...]).astype(BF16)
    for hd in range(MLA_HEADS):
        c0 = hd * (MLA_NOPE + MLA_V)
        kn = kv[:, c0:c0 + MLA_NOPE]
        knn = kn * lax.rsqrt(_group_ms(kn, m128_ref) + NORM_EPS) * gn_ref[...]
        ko_ref[:, hd * MLA_QK_PAD:hd * MLA_QK_PAD + LANE] = knn.astype(BF16)
        ko_ref[:, hd * MLA_QK_PAD + LANE:(hd + 1) * MLA_QK_PAD] = krr
        vo_ref[:, hd * MLA_V:(hd + 1) * MLA_V] = kv[:, c0 + MLA_NOPE:c0 + MLA_NOPE + MLA_V].astype(BF16)


def _mla_k_call(P, an, w, gn, gr, m128, m64, tabs):
    NT = P.shape[0]
    WK = MLA_HEADS * MLA_QK_PAD
    WV = MLA_HEADS * MLA_V
    tab = pl.BlockSpec((TM, LANE), lambda i: (i, 0))
    vec = pl.BlockSpec((1, LANE), lambda i: (0, 0))
    mat = pl.BlockSpec((LANE, LANE), lambda i: (0, 0))
    return pl.pallas_call(
        _mla_k_kernel,
        grid=(NT // TM,),
        in_specs=[pl.BlockSpec((TM, KV_LORA), lambda i: (i, COL_CKV // KV_LORA)),
                  pl.BlockSpec((TM, LANE), lambda i: (i, COL_KROPE // LANE)),
                  pl.BlockSpec((1, KV_LORA), lambda i: (0, 0)),
                  pl.BlockSpec((KV_LORA, MLA_HEADS * (MLA_NOPE + MLA_V)), lambda i: (0, 0)),
                  vec, vec, mat, mat, tab, tab, tab],
        out_specs=[pl.BlockSpec((TM, WK), lambda i: (i, 0)), pl.BlockSpec((TM, WV), lambda i: (i, 0))],
        out_shape=[jax.ShapeDtypeStruct((NT, WK), BF16), jax.ShapeDtypeStruct((NT, WV), BF16)],
        compiler_params=_cparams(1),
        name="mla_k_prep",
    )(P, P, an, w, gn, gr, m128, m64, *tabs)


_NT_DIMS = (((1,), (1,)), ((), ()))


def _attend_online(qz, kc_ref, vc_ref, kl_ref, vl_ref, n_lat):
    s = lax.dot_general(qz, kc_ref[...], _NT_DIMS, preferred_element_type=F32)
    m = jnp.max(s, axis=1, keepdims=True)
    p = jnp.exp2(s - m)
    l = jnp.sum(p, axis=1, keepdims=True)
    acc = jnp.dot(p.astype(BF16), vc_ref[...], preferred_element_type=F32)
    if n_lat == 0:
        return l, acc

    def body(j, carry):
        m, l, acc = carry
        off = pl.multiple_of(j * TK, TK)
        s = lax.dot_general(qz, kl_ref[pl.ds(off, TK), :], _NT_DIMS, preferred_element_type=F32)
        m_new = jnp.maximum(m, jnp.max(s, axis=1, keepdims=True))
        alpha = jnp.exp2(m - m_new)
        p = jnp.exp2(s - m_new)
        l = alpha * l + jnp.sum(p, axis=1, keepdims=True)
        acc = alpha * acc + jnp.dot(p.astype(BF16), vl_ref[pl.ds(off, TK), :], preferred_element_type=F32)
        return m_new, l, acc

    _, l, acc = lax.fori_loop(0, n_lat // TK, body, (m, l, acc))
    return l, acc


def _attend_fixed(qz, kc_ref, vc_ref, kl_ref, vl_ref, n_lat, shift):
    def chunk(k, v):
        s = lax.dot_general(qz, k, _NT_DIMS, preferred_element_type=F32)
        p = jnp.exp2(s - shift)
        lp = p[:, :LANE]
        for t in range(1, p.shape[1] // LANE):
            lp = lp + p[:, t * LANE:(t + 1) * LANE]
        return lp, jnp.dot(p.astype(BF16), v, preferred_element_type=F32)

    lp, acc = chunk(kc_ref[...], vc_ref[...])
    if n_lat:
        tk = math.gcd(n_lat, TK_FIXED)

        def body(j, carry):
            lp, acc = carry
            off = pl.multiple_of(j * tk, tk)
            dl, da = chunk(kl_ref[pl.ds(off, tk), :], vl_ref[pl.ds(off, tk), :])
            return lp + dl, acc + da

        lp, acc = lax.fori_loop(0, n_lat // tk, body, (lp, acc))
    return jnp.sum(lp, axis=1, keepdims=True), acc


def _attend(qz, ctl_ref, kc_ref, vc_ref, kl_ref, vl_ref, n_lat, finish):
    fixed = ctl_ref[1] > 0.5

    @pl.when(fixed)
    def _():
        finish(*_attend_fixed(qz, kc_ref, vc_ref, kl_ref, vl_ref, n_lat, ctl_ref[0]))

    @pl.when(jnp.logical_not(fixed))
    def _():
        finish(*_attend_online(qz, kc_ref, vc_ref, kl_ref, vl_ref, n_lat))


def _da_attn_kernel(*refs, n_lat, lam_init):
    if n_lat:
        ctl_ref, q_ref, kc_ref, vc_ref, kl_ref, vl_ref, lam_ref, sub_ref, o_ref = refs
    else:
        ctl_ref, q_ref, kc_ref, vc_ref, lam_ref, sub_ref, _, o_ref = refs
        kl_ref = vl_ref = None
    q = q_ref[...]
    tq = q.shape[0]
    lane = lax.broadcasted_iota(jnp.int32, q.shape, 1)
    zero = jnp.zeros_like(q)
    qz = jnp.concatenate([jnp.where(lane < DA_HEAD, q, zero), jnp.where(lane >= DA_HEAD, q, zero)], axis=0)

    def finish(l, acc):
        o = acc / l
        lamp = lam_ref[...]
        lam = (jnp.exp(jnp.sum(lamp[0:1, :] * lamp[1:2, :], axis=1, keepdims=True))
               - jnp.exp(jnp.sum(lamp[2:3, :] * lamp[3:4, :], axis=1, keepdims=True)) + lam_init)
        d = o[:tq, :] - lam * o[tq:, :]
        ms = jnp.mean(d * d, axis=-1, keepdims=True)
        o_ref[...] = (d * lax.rsqrt(ms + NORM_EPS) * sub_ref[...] * (1.0 - lam_init)).astype(BF16)

    _attend(qz, ctl_ref, kc_ref, vc_ref, kl_ref, vl_ref, n_lat, finish)


def _mla_attn_kernel(*refs, n_lat):
    if n_lat:
        ctl_ref, q_ref, kc_ref, vc_ref, kl_ref, vl_ref, o_ref = refs
    else:
        ctl_ref, q_ref, kc_ref, vc_ref, _, o_ref = refs
        kl_ref = vl_ref = None

    def finish(l, acc):
        o_ref[...] = (acc / l).astype(BF16)

    _attend(q_ref[...], ctl_ref, kc_ref, vc_ref, kl_ref, vl_ref, n_lat, finish)


def _amax(g):
    return jnp.max(jnp.abs(g))


def _mla_norm_bound(g):
    return jnp.sqrt(MLA_NOPE * _amax(g[:MLA_NOPE]) ** 2 + MLA_ROPE * _amax(g[MLA_NOPE:]) ** 2)


def _softmax_ctl(bound):
    return jnp.stack([bound, (bound <= FAST_MAX_SHIFT).astype(F32)]).astype(F32)


def _attn_calls(kernel_fn, name, ctl, q_arr, k_arr, v_arr, extra, dq, dv, k_col0, v_col0, tq, S, B, CTX, heads,
                with_ctx):
    NT = q_arr.shape[0]
    ctx_blk0 = (B * S) // CTX
    smem = pl.BlockSpec(memory_space=pltpu.SMEM)
    extra_specs = [pl.BlockSpec(e.shape, lambda b, h, i: (0, 0)) for e in extra]
    nq = S // tq
    out_shape = jax.ShapeDtypeStruct((NT, heads * dv), BF16)
    kc = pl.BlockSpec((CTX, dq), lambda b, h, i: (ctx_blk0 + b, k_col0 // dq + h))
    vc = pl.BlockSpec((CTX, dv), lambda b, h, i: (ctx_blk0 + b, v_col0 // dv + h))
    out = pl.pallas_call(
        functools.partial(kernel_fn, n_lat=S),
        grid=(B, heads, nq),
        in_specs=[smem, pl.BlockSpec((tq, dq), lambda b, h, i: (b * nq + i, h)), kc, vc,
                  pl.BlockSpec((S, dq), lambda b, h, i: (b, k_col0 // dq + h)),
                  pl.BlockSpec((S, dv), lambda b, h, i: (b, v_col0 // dv + h))] + extra_specs,
        out_specs=pl.BlockSpec((tq, dv), lambda b, h, i: (b * nq + i, h)),
        out_shape=out_shape,
        compiler_params=_cparams(3),
        name=name,
    )(ctl, q_arr, k_arr, v_arr, k_arr, v_arr, *extra)
    if not with_ctx:
        return out
    n_in = 4 + len(extra)
    return pl.pallas_call(
        functools.partial(kernel_fn, n_lat=0),
        grid=(B, heads, 1),
        in_specs=[smem, pl.BlockSpec((CTX, dq), lambda b, h, i: (ctx_blk0 + b, h)), kc, vc] + extra_specs
                 + [pl.BlockSpec(memory_space=pl.ANY)],
        out_specs=pl.BlockSpec((CTX, dv), lambda b, h, i: (ctx_blk0 + b, h)),
        out_shape=out_shape,
        input_output_aliases={n_in: 0},
        compiler_params=_cparams(3),
        name=name + "_ctx",
    )(ctl, q_arr, k_arr, v_arr, *extra, out)


def _merge_kernel(yc_ref, od_ref, om_ref, gc_ref, gd_ref, gm_ref, wc_ref, wd_ref, wm_ref, o_ref):
    def branch(a_ref, g_ref, w_ref):
        g = g_ref[...].astype(F32)
        return (1.0 / (1.0 + jnp.exp(-g))) * jnp.dot(a_ref[...], w_ref[...], preferred_element_type=F32)

    o_ref[...] = (branch(yc_ref, gc_ref, wc_ref) + branch(od_ref, gd_ref, wd_ref)
                  + branch(om_ref, gm_ref, wm_ref)).astype(BF16)


def _merge_call(yc, od, om, P, wc, wd, wm, n_rows, D):
    tn = 1024
    act = pl.BlockSpec((TM, CONV_W), lambda j, i: (i, 0))

    def gate(k):
        return pl.BlockSpec((TM, tn), lambda j, i: (i, (COL_GATES + k * D) // tn + j))

    wsp = pl.BlockSpec((CONV_W, tn), lambda j, i: (0, j))
    return pl.pallas_call(
        _merge_kernel,
        grid=(D // tn, n_rows // TM),
        in_specs=[act, act, act, gate(0), gate(1), gate(2), wsp, wsp, wsp],
        out_specs=pl.BlockSpec((TM, tn), lambda j, i: (i, j)),
        out_shape=jax.ShapeDtypeStruct((n_rows, D), BF16),
        compiler_params=_cparams(2),
        name="branch_merge",
    )(yc, od, om, P, P, P, wc, wd, wm)


def _oproj_kernel(y_ref, w_ref, h_ref, g1_ref, n2_ref, sh2_ref, sc2_ref, ho_ref, xo_ref):
    h1 = h_ref[...] + g1_ref[...] * jnp.dot(y_ref[...], w_ref[...], preferred_element_type=F32)
    ho_ref[...] = h1
    xo_ref[...] = _norm_mod(h1, n2_ref[...], sh2_ref[...], sc2_ref[...]).astype(BF16)


def _oproj_call(y, w_o, h, n2, modv, l, n_rows, S, B):
    D = w_o.shape[0]
    tm = TM_OPROJ
    row = pl.BlockSpec((tm, D), lambda i: (i, 0))
    return pl.pallas_call(
        _oproj_kernel,
        grid=(n_rows // tm,),
        in_specs=[row, pl.BlockSpec((D, D), lambda i: (0, 0)), row,
                  _mod_spec(l, 2, D, tm, S, B),
                  pl.BlockSpec((1, D), lambda i: (0, 0)),
                  _mod_spec(l, 3, D, tm, S, B), _mod_spec(l, 4, D, tm, S, B)],
        out_specs=[row, row],
        out_shape=[jax.ShapeDtypeStruct((n_rows, D), F32), jax.ShapeDtypeStruct((n_rows, D), BF16)],
        compiler_params=_cparams(1),
        name="out_proj",
    )(y, w_o, h, modv, n2, modv, modv)


def _ffn_kernel(*refs, emit_xn):
    if emit_xn:
        x_ref, w1_ref, w2_ref, h_ref, g2_ref, n_ref, sh_ref, sc_ref, ho_ref, xo_ref, acc_ref = refs
    else:
        x_ref, w1_ref, w2_ref, h_ref, g2_ref, ho_ref, acc_ref = refs
    f = pl.program_id(1)

    @pl.when(f == 0)
    def _():
        acc_ref[...] = jnp.zeros_like(acc_ref)

    hid = jnp.maximum(jnp.dot(x_ref[...], w1_ref[...], preferred_element_type=F32), 0.0)
    acc_ref[...] += jnp.dot((hid * hid).astype(BF16), w2_ref[...], preferred_element_type=F32)

    @pl.when(f == pl.num_programs(1) - 1)
    def _():
        h2 = h_ref[...] + g2_ref[...] * acc_ref[...]
        ho_ref[...] = h2
        if emit_xn:
            xo_ref[...] = _norm_mod(h2, n_ref[...], sh_ref[...], sc_ref[...]).astype(BF16)


def _ffn_call(xn, w1, w2, h, modv, l, n_rows, S, B, n_next):
    D, FF = w1.shape
    tf = 512
    emit_xn = n_next is not None
    row = pl.BlockSpec((TM, D), lambda i, f: (i, 0))
    in_specs = [row, pl.BlockSpec((D, tf), lambda i, f: (0, f)), pl.BlockSpec((tf, D), lambda i, f: (f, 0)),
                row, _mod_spec(l, 5, D, TM, S, B)]
    args = [xn, w1, w2, h, modv]
    out_specs = [row]
    out_shape = [jax.ShapeDtypeStruct((n_rows, D), F32)]
    if emit_xn:
        in_specs += [pl.BlockSpec((1, D), lambda i, f: (0, 0)),
                     _mod_spec(l + 1, 0, D, TM, S, B), _mod_spec(l + 1, 1, D, TM, S, B)]
        args += [n_next, modv, modv]
        out_specs.append(row)
        out_shape.append(jax.ShapeDtypeStruct((n_rows, D), BF16))
    res = pl.pallas_call(
        functools.partial(_ffn_kernel, emit_xn=emit_xn),
        grid=(n_rows // TM, FF // tf),
        in_specs=in_specs,
        out_specs=out_specs,
        out_shape=out_shape,
        scratch_shapes=[pltpu.VMEM((TM, D), F32)],
        compiler_params=_cparams(2),
        name="ffn",
    )(*args)
    return (res[0], res[1]) if emit_xn else (res[0], None)


def _rope_tables(B, S, CTX):
    half = DA_HEAD // 2
    inv_freq = ROPE_THETA ** (-jnp.arange(0, half, 2, dtype=F32) / half)
    s_idx = jnp.arange(S, dtype=jnp.int32)
    rows = (s_idx // GRID_W).astype(F32)
    cols = (s_idx % GRID_W).astype(F32)
    ang_r = rows[:, None] * inv_freq[None, :]
    ang_c = cols[:, None] * inv_freq[None, :]
    ang = jnp.concatenate([ang_r, ang_r, ang_c, ang_c], axis=1)
    cos, sin = jnp.cos(ang), jnp.sin(ang)
    first = (jnp.arange(DA_HEAD) % half) < (half // 2)
    sa = jnp.where(first[None, :], -sin, 0.0)
    sb = jnp.where(first[None, :], 0.0, sin)

    def full(t, ctx_val):
        t = jnp.tile(jnp.concatenate([t, t], axis=1), (B, 1))
        return jnp.concatenate([t, jnp.full((B * CTX, LANE), ctx_val, F32)], axis=0)

    return full(cos, 1.0), full(sa, 0.0), full(sb, 0.0)


def _avg_matrix(groups):
    n = LANE * ((max(s + w for s, w in groups) + LANE - 1) // LANE)
    idx = jnp.arange(n)
    m = jnp.zeros((n, n), F32)
    for s, w in groups:
        inside = (idx >= s) & (idx < s + w)
        m = m + jnp.where(inside[:, None] & inside[None, :], 1.0 / w, 0.0)
    return m.astype(BF16)


def kernel(x, c, ctx, c_ctx, w_mod, b_mod, norm1_w, norm2_w, w_in, conv_w, da_q_norm, da_k_norm, da_lambda,
           da_subln, mla_q_a_norm, w_q_b, mla_kv_a_norm, w_kv_b, mla_q_norm, mla_k_norm, w_conv_out, w_da_out,
           w_mla_out, w_o, w_mlp1, w_mlp2):
    B, S, D = x.shape
    CTX = ctx.shape[1]
    L = w_mod.shape[0]
    NL = B * S
    NT = NL + B * CTX
    assert S % TM == 0 and S % TK == 0 and S % GRID_W == 0 and (B * CTX) % TM == 0 and CTX == T_CONV
    assert COL_KROPE + LANE <= COL_GATES and D % 1024 == 0 and B + 1 <= 16

    h = jnp.concatenate([x.reshape(NL, D), ctx.reshape(B * CTX, D)], axis=0)
    cin = jnp.concatenate([c, c_ctx[None, :], jnp.zeros((16 - B - 1, D), F32)], axis=0)
    modv = _mod_call(cin, w_mod, b_mod)[:, :B + 1].reshape(L, B + 1, 1, 6 * D)

    tabs = _rope_tables(B, S, CTX)
    m64x2 = _avg_matrix([(0, DA_HEAD), (DA_HEAD, DA_HEAD)])
    m256 = _avg_matrix([(0, MLA_NOPE), (MLA_NOPE, MLA_ROPE)])
    m128 = _avg_matrix([(0, MLA_NOPE)])
    m64 = _avg_matrix([(0, MLA_ROPE)])

    q_end = 3 * CONV_W + DA_HEADS * 2 * DA_HEAD
    cq_end = q_end + Q_LORA
    kv0 = cq_end + N_BRANCH * D
    dkv_end = kv0 + DA_HEADS * 2 * DA_HEAD + DA_HEADS * DA_V

    xn = _norm_mod_call(h, norm1_w[0][None, :], modv, 0, S, B)
    for l in range(L):
        last = l == L - 1
        n_rows = NL if last else NT
        lam_init = 0.8 - 0.6 * math.exp(-0.3 * l)

        w = w_in[l].astype(BF16)
        w_perm = jnp.concatenate(
            [w[:, :q_end], w[:, kv0:dkv_end], w[:, q_end:cq_end], w[:, dkv_end:],
             jnp.zeros((D, COL_GATES - COL_KROPE - MLA_ROPE), BF16), w[:, cq_end:kv0]], axis=1)
        wqb = jnp.pad(w_q_b[l].astype(BF16).reshape(Q_LORA, MLA_HEADS, MLA_QK),
                      ((0, 0), (0, 0), (0, MLA_QK_PAD - MLA_QK))).reshape(Q_LORA, MLA_HEADS * MLA_QK_PAD)
        zpad = jnp.zeros((MLA_QK_PAD - MLA_QK,), F32)
        qg = jnp.tile(da_q_norm[l], 2)[None, :]
        kg = jnp.tile(da_k_norm[l], 2)[None, :]
        mq_g = jnp.concatenate([mla_q_norm[l], zpad])[None, :]
        mk_gn = mla_k_norm[l][None, :MLA_NOPE]
        mk_gr = jnp.concatenate([mla_k_norm[l][MLA_NOPE:], zpad])[None, :]

        P = _proj_call(xn, w_perm)
        yc = _conv_call(P, conv_w[l], n_rows, S, B, CTX)
        q_da, k_da = _da_prep_call(P, tabs, qg, kg, m64x2)
        q_m = _mla_q_call(P, mla_q_a_norm[l][None, :], wqb, mq_g, m256, tabs)
        k_m, v_m = _mla_k_call(P, mla_kv_a_norm[l][None, :], w_kv_b[l].astype(BF16), mk_gn, mk_gr, m128, m64, tabs)

        da_ctl = _softmax_ctl(DA_SCALE * LOG2E * DA_HEAD * _amax(da_q_norm[l]) * _amax(da_k_norm[l]))
        mla_ctl = _softmax_ctl(MLA_SCALE * LOG2E * _mla_norm_bound(mla_q_norm[l]) * _mla_norm_bound(mla_k_norm[l]))
        o_da = _attn_calls(functools.partial(_da_attn_kernel, lam_init=lam_init), "da_attn", da_ctl,
                           q_da, k_da, P, [da_lambda[l], da_subln[l][None, :]],
                           2 * DA_HEAD, DA_V, 0, COL_DA_V, TQ_DA, S, B, CTX, DA_HEADS, not last)
        o_m = _attn_calls(_mla_attn_kernel, "mla_attn", mla_ctl, q_m, k_m, v_m, [],
                          MLA_QK_PAD, MLA_V, 0, 0, TQ_MLA, S, B, CTX, MLA_HEADS, not last)

        y = _merge_call(yc, o_da, o_m, P, w_conv_out[l].astype(BF16), w_da_out[l].astype(BF16),
                        w_mla_out[l].astype(BF16), n_rows, D)
        h1, xn2 = _oproj_call(y, w_o[l].astype(BF16), h, norm2_w[l][None, :], modv, l, n_rows, S, B)
        h, xn = _ffn_call(xn2, w_mlp1[l].astype(BF16), w_mlp2[l].astype(BF16), h1, modv, l, n_rows, S, B,
                          None if last else norm1_w[l + 1][None, :])
    return h.reshape(B, S, D)
```

```python
import functools
import math

import jax
import jax.numpy as jnp
from jax import lax
from jax.experimental import pallas as pl
from jax.experimental.pallas import tpu as pltpu

F32 = jnp.float32
BF16 = jnp.bfloat16

GRID_W = 64
ROPE_THETA = 10000.0
NORM_EPS = 1e-6
CONV_W = 1024
CONV_K = 3
DA_HEADS = 8
DA_HEAD = 64
DA_V = 2 * DA_HEAD
DA_SCALE = DA_HEAD ** -0.5
MLA_HEADS = 8
MLA_NOPE = 128
MLA_ROPE = 64
MLA_V = 128
MLA_QK = MLA_NOPE + MLA_ROPE
MLA_QK_PAD = 256
Q_LORA = 512
KV_LORA = 512
MLA_SCALE = MLA_QK ** -0.5
N_BRANCH = 3
LOG2E = math.log2(math.e)
FAST_MAX_SHIFT = 50.0

LANE = 128
SUBLANE = 8

COL_CONV_B = 0
COL_CONV_C = CONV_W
COL_CONV_H = 2 * CONV_W
COL_DA_Q = 3 * CONV_W
COL_DA_K = COL_DA_Q + DA_HEADS * 2 * DA_HEAD
COL_DA_V = COL_DA_K + DA_HEADS * 2 * DA_HEAD
COL_CQ = COL_DA_V + DA_HEADS * DA_V
COL_CKV = COL_CQ + Q_LORA
COL_KROPE = COL_CKV + KV_LORA
COL_GATES = 8192

TM = 512
TM_OPROJ = 256
T_CONV = 256
TQ_DA = 512
TQ_MLA = 1024
TK = 512
TK_FIXED = 8192
TK_SUB = 1024
VMEM_LIMIT = 52 * 1024 * 1024


def _cparams(n_axes, vmem=VMEM_LIMIT):
    return pltpu.CompilerParams(dimension_semantics=("arbitrary",) * n_axes, vmem_limit_bytes=vmem)


def _norm_mod(h, g, sh, sc):
    ms = jnp.mean(h * h, axis=-1, keepdims=True)
    return (h * lax.rsqrt(ms + NORM_EPS) * g) * (1.0 + sc) + sh


def _group_ms(x, m_ref):
    return jnp.dot((x * x).astype(BF16), m_ref[...], preferred_element_type=F32)


def _rope(r, cos, sa, sb):
    return r * cos + pltpu.roll(r, LANE - 16, 1) * sa + pltpu.roll(r, 16, 1) * sb


def _mod_kernel(c_ref, w_ref, b_ref, o_ref):
    c = c_ref[...]
    s = c * (1.0 / (1.0 + jnp.exp(-c)))
    o_ref[...] = jnp.dot(s.astype(BF16), w_ref[...].astype(BF16), preferred_element_type=F32) + b_ref[...]


def _mod_call(cin, w_mod, b_mod):
    L, D, N = w_mod.shape
    tn = 1024
    return pl.pallas_call(
        _mod_kernel,
        grid=(L, N // tn),
        in_specs=[pl.BlockSpec((16, D), lambda l, j: (0, 0)),
                  pl.BlockSpec((None, D, tn), lambda l, j: (l, 0, j)),
                  pl.BlockSpec((None, 1, tn), lambda l, j: (l, 0, j))],
        out_specs=pl.BlockSpec((None, 16, tn), lambda l, j: (l, 0, j)),
        out_shape=jax.ShapeDtypeStruct((L, 16, N), F32),
        compiler_params=_cparams(2),
        name="adaln_mod",
    )(cin, w_mod, b_mod.reshape(L, 1, N))


def _mod_spec(l, k, D, tm, S, B):
    return pl.BlockSpec((None, None, 1, D), lambda i, *_: (l, jnp.minimum(i * tm // S, B), 0, k))


def _stream_specs(hs, tm):
    D = hs[0].shape[1]
    if len(hs) == 1:
        return [pl.BlockSpec((tm, D), lambda i, *_: (i, 0))]
    n0 = hs[0].shape[0] // tm
    n1 = hs[1].shape[0] // tm
    return [pl.BlockSpec((tm, D), lambda i, *_: (jnp.minimum(i, n0 - 1), 0)),
            pl.BlockSpec((tm, D), lambda i, *_: (jnp.clip(i - n0, 0, n1 - 1), 0))]


def _norm_mod_kernel(*refs, n_lat_tiles):
    *h_refs, g_ref, sh_ref, sc_ref, o_ref = refs
    h = _select_stream(h_refs, n_lat_tiles)
    o_ref[...] = _norm_mod(h, g_ref[...], sh_ref[...], sc_ref[...]).astype(BF16)


def _select_stream(h_refs, n_lat_tiles):
    if len(h_refs) == 1:
        return h_refs[0][...]
    return jnp.where(pl.program_id(0) < n_lat_tiles, h_refs[0][...], h_refs[1][...])


def _norm_mod_call(hs, g, modv, l, S, B):
    NT = sum(h.shape[0] for h in hs)
    D = hs[0].shape[1]
    return pl.pallas_call(
        functools.partial(_norm_mod_kernel, n_lat_tiles=hs[0].shape[0] // TM),
        grid=(NT // TM,),
        in_specs=_stream_specs(hs, TM) + [pl.BlockSpec((1, D), lambda i: (0, 0)),
                                           _mod_spec(l, 0, D, TM, S, B),
                                           _mod_spec(l, 1, D, TM, S, B)],
        out_specs=pl.BlockSpec((TM, D), lambda i: (i, 0)),
        out_shape=jax.ShapeDtypeStruct((NT, D), BF16),
        compiler_params=_cparams(1),
        name="norm_mod",
    )(*hs, g, modv, modv)


def _proj_kernel(x_ref, w_ref, o_ref):
    o_ref[...] = jnp.dot(x_ref[...], w_ref[...], preferred_element_type=F32).astype(BF16)


def _proj_call(xn, w):
    NT, D = xn.shape
    NP = w.shape[1]
    tn = 2048 if NP % 2048 == 0 else 1024
    return pl.pallas_call(
        _proj_kernel,
        grid=(NP // tn, NT // TM),
        in_specs=[pl.BlockSpec((TM, D), lambda j, i: (i, 0)),
                  pl.BlockSpec((D, tn), lambda j, i: (0, j))],
        out_specs=pl.BlockSpec((TM, tn), lambda j, i: (i, j)),
        out_shape=jax.ShapeDtypeStruct((NT, NP), BF16),
        compiler_params=_cparams(2),
        name="in_proj",
    )(xn, w)


def _conv_kernel(b_ref, c_ref, h_ref, cp_ref, hp_ref, cn_ref, hn_ref, w_ref, o_ref, *, starts, ends):
    i = pl.program_id(0)
    u = c_ref[...].astype(F32) * h_ref[...].astype(F32)
    up = cp_ref[SUBLANE - 1:SUBLANE, :].astype(F32) * hp_ref[SUBLANE - 1:SUBLANE, :].astype(F32)
    un = cn_ref[0:1, :].astype(F32) * hn_ref[0:1, :].astype(F32)
    is_start = functools.reduce(jnp.logical_or, [i == s for s in starts])
    is_end = functools.reduce(jnp.logical_or, [i == e for e in ends])
    up = jnp.where(is_start, 0.0, up)
    un = jnp.where(is_end, 0.0, un)
    t = u.shape[0]
    row = lax.broadcasted_iota(jnp.int32, u.shape, 0)
    u_dn = jnp.where(row == 0, up, pltpu.roll(u, 1, 0))
    u_up = jnp.where(row == t - 1, un, pltpu.roll(u, t - 1, 0))
    w = w_ref[...]
    y = w[0:1, :] * u_dn + w[1:2, :] * u + w[2:3, :] * u_up
    o_ref[...] = (b_ref[...].astype(F32) * y).astype(BF16)


def _conv_call(P, conv_w, n_rows, S, B, CTX):
    NT = P.shape[0]
    t = T_CONV
    r8 = t // SUBLANE
    cb = CONV_W // CONV_W
    starts = [b * S // t for b in range(B)] + [(B * S + b * CTX) // t for b in range(B)]
    ends = [(b + 1) * S // t - 1 for b in range(B)] + [(B * S + (b + 1) * CTX) // t - 1 for b in range(B)]
    last8 = NT // SUBLANE - 1

    def main(col):
        return pl.BlockSpec((t, CONV_W), lambda i: (i, col // CONV_W))

    def prev(col):
        return pl.BlockSpec((SUBLANE, CONV_W), lambda i: (jnp.maximum(i * r8 - 1, 0), col // CONV_W))

    def nxt(col):
        return pl.BlockSpec((SUBLANE, CONV_W), lambda i: (jnp.minimum((i + 1) * r8, last8), col // CONV_W))

    del cb
    return pl.pallas_call(
        functools.partial(_conv_kernel, starts=starts, ends=ends),
        grid=(n_rows // t,),
        in_specs=[main(COL_CONV_B), main(COL_CONV_C), main(COL_CONV_H),
                  prev(COL_CONV_C), prev(COL_CONV_H), nxt(COL_CONV_C), nxt(COL_CONV_H),
                  pl.BlockSpec((CONV_K, CONV_W), lambda i: (0, 0))],
        out_specs=pl.BlockSpec((t, CONV_W), lambda i: (i, 0)),
        out_shape=jax.ShapeDtypeStruct((n_rows, CONV_W), BF16),
        compiler_params=_cparams(1),
        name="short_conv",
    )(P, P, P, P, P, P, P, conv_w)


def _da_prep_kernel(q_ref, k_ref, cos_ref, sa_ref, sb_ref, qg_ref, kg_ref, m_ref, qo_ref, ko_ref):
    cos, sa, sb = cos_ref[...], sa_ref[...], sb_ref[...]
    for hd in range(DA_HEADS):
        sl = slice(hd * LANE, (hd + 1) * LANE)
        for src, g_ref, dst, scale in ((q_ref, qg_ref, qo_ref, DA_SCALE * LOG2E), (k_ref, kg_ref, ko_ref, 1.0)):
            x = src[:, sl].astype(F32)
            xn = x * lax.rsqrt(_group_ms(x, m_ref) + NORM_EPS) * g_ref[...]
            dst[:, sl] = (_rope(xn, cos, sa, sb) * scale).astype(BF16)


def _da_prep_call(P, tabs, qg, kg, m64x2):
    NT = P.shape[0]
    W = DA_HEADS * 2 * DA_HEAD
    tab = pl.BlockSpec((TM, LANE), lambda i: (i, 0))
    vec = pl.BlockSpec((1, LANE), lambda i: (0, 0))
    return pl.pallas_call(
        _da_prep_kernel,
        grid=(NT // TM,),
        in_specs=[pl.BlockSpec((TM, W), lambda i: (i, COL_DA_Q // W)),
                  pl.BlockSpec((TM, W), lambda i: (i, COL_DA_K // W)),
                  tab, tab, tab, vec, vec,
                  pl.BlockSpec((LANE, LANE), lambda i: (0, 0))],
        out_specs=[pl.BlockSpec((TM, W), lambda i: (i, 0)), pl.BlockSpec((TM, W), lambda i: (i, 0))],
        out_shape=[jax.ShapeDtypeStruct((NT, W), BF16), jax.ShapeDtypeStruct((NT, W), BF16)],
        compiler_params=_cparams(1),
        name="da_prep",
    )(P, P, *tabs, qg, kg, m64x2)


def _mla_q_kernel(cq_ref, an_ref, w_ref, g_ref, m_ref, cos_ref, sa_ref, sb_ref, o_ref):
    cos, sa, sb = cos_ref[...], sa_ref[...], sb_ref[...]
    x = cq_ref[...].astype(F32)
    xn = x * lax.rsqrt(jnp.mean(x * x, axis=-1, keepdims=True) + NORM_EPS) * an_ref[...]
    q = jnp.dot(xn.astype(BF16), w_ref[...], preferred_element_type=F32)
    for hd in range(MLA_HEADS):
        c0 = hd * MLA_QK_PAD
        qh = q[:, c0:c0 + MLA_QK_PAD]
        qn = qh * lax.rsqrt(_group_ms(qh, m_ref) + NORM_EPS) * g_ref[...]
        o_ref[:, c0:c0 + LANE] = (qn[:, :LANE] * (MLA_SCALE * LOG2E)).astype(BF16)
        o_ref[:, c0 + LANE:c0 + 2 * LANE] = (_rope(qn[:, LANE:], cos, sa, sb) * (MLA_SCALE * LOG2E)).astype(BF16)


def _mla_q_call(P, an, w, g, m256, tabs):
    NT = P.shape[0]
    W = MLA_HEADS * MLA_QK_PAD
    tab = pl.BlockSpec((TM, LANE), lambda i: (i, 0))
    return pl.pallas_call(
        _mla_q_kernel,
        grid=(NT // TM,),
        in_specs=[pl.BlockSpec((TM, Q_LORA), lambda i: (i, COL_CQ // Q_LORA)),
                  pl.BlockSpec((1, Q_LORA), lambda i: (0, 0)),
                  pl.BlockSpec((Q_LORA, W), lambda i: (0, 0)),
                  pl.BlockSpec((1, MLA_QK_PAD), lambda i: (0, 0)),
                  pl.BlockSpec((MLA_QK_PAD, MLA_QK_PAD), lambda i: (0, 0)),
                  tab, tab, tab],
        out_specs=pl.BlockSpec((TM, W), lambda i: (i, 0)),
        out_shape=jax.ShapeDtypeStruct((NT, W), BF16),
        compiler_params=_cparams(1),
        name="mla_q_prep",
    )(P, an, w, g, m256, *tabs)


def _mla_k_kernel(ckv_ref, kr_ref, an_ref, w_ref, gn_ref, gr_ref, m128_ref, m64_ref,
                  cos_ref, sa_ref, sb_ref, ko_ref, vo_ref):
    x = ckv_ref[...].astype(F32)
    xn = x * lax.rsqrt(jnp.mean(x * x, axis=-1, keepdims=True) + NORM_EPS) * an_ref[...]
    kv = jnp.dot(xn.astype(BF16), w_ref[...], preferred_element_type=F32)
    kr = kr_ref[...].astype(F32)
    krn = kr * lax.rsqrt(_group_ms(kr, m64_ref) + NORM_EPS) * gr_ref[...]
    krr = _rope(krn, cos_ref[...], sa_ref[...], sb_ref[...]).astype(BF16)
    for hd in range(MLA_HEADS):
        c0 = hd * (MLA_NOPE + MLA_V)
        kn = kv[:, c0:c0 + MLA_NOPE]
        knn = kn * lax.rsqrt(_group_ms(kn, m128_ref) + NORM_EPS) * gn_ref[...]
        ko_ref[:, hd * MLA_QK_PAD:hd * MLA_QK_PAD + LANE] = knn.astype(BF16)
        ko_ref[:, hd * MLA_QK_PAD + LANE:(hd + 1) * MLA_QK_PAD] = krr
        vo_ref[:, hd * MLA_V:(hd + 1) * MLA_V] = kv[:, c0 + MLA_NOPE:c0 + MLA_NOPE + MLA_V].astype(BF16)


def _mla_k_call(P, an, w, gn, gr, m128, m64, tabs):
    NT = P.shape[0]
    WK = MLA_HEADS * MLA_QK_PAD
    WV = MLA_HEADS * MLA_V
    tab = pl.BlockSpec((TM, LANE), lambda i: (i, 0))
    vec = pl.BlockSpec((1, LANE), lambda i: (0, 0))
    mat = pl.BlockSpec((LANE, LANE), lambda i: (0, 0))
    return pl.pallas_call(
        _mla_k_kernel,
        grid=(NT // TM,),
        in_specs=[pl.BlockSpec((TM, KV_LORA), lambda i: (i, COL_CKV // KV_LORA)),
                  pl.BlockSpec((TM, LANE), lambda i: (i, COL_KROPE // LANE)),
                  pl.BlockSpec((1, KV_LORA), lambda i: (0, 0)),
                  pl.BlockSpec((KV_LORA, MLA_HEADS * (MLA_NOPE + MLA_V)), lambda i: (0, 0)),
                  vec, vec, mat, mat, tab, tab, tab],
        out_specs=[pl.BlockSpec((TM, WK), lambda i: (i, 0)), pl.BlockSpec((TM, WV), lambda i: (i, 0))],
        out_shape=[jax.ShapeDtypeStruct((NT, WK), BF16), jax.ShapeDtypeStruct((NT, WV), BF16)],
        compiler_params=_cparams(1),
        name="mla_k_prep",
    )(P, P, an, w, gn, gr, m128, m64, *tabs)


_NT_DIMS = (((1,), (1,)), ((), ()))


def _attend_online(qz, kc_ref, vc_ref, kl_ref, vl_ref, n_lat):
    s = lax.dot_general(qz, kc_ref[...], _NT_DIMS, preferred_element_type=F32)
    m = jnp.max(s, axis=1, keepdims=True)
    p = jnp.exp2(s - m)
    l = jnp.sum(p, axis=1, keepdims=True)
    acc = jnp.dot(p.astype(BF16), vc_ref[...], preferred_element_type=F32)
    if n_lat == 0:
        return l, acc

    def body(j, carry):
        m, l, acc = carry
        off = pl.multiple_of(j * TK, TK)
        s = lax.dot_general(qz, kl_ref[pl.ds(off, TK), :], _NT_DIMS, preferred_element_type=F32)
        m_new = jnp.maximum(m, jnp.max(s, axis=1, keepdims=True))
        alpha = jnp.exp2(m - m_new)
        p = jnp.exp2(s - m_new)
        l = alpha * l + jnp.sum(p, axis=1, keepdims=True)
        acc = alpha * acc + jnp.dot(p.astype(BF16), vl_ref[pl.ds(off, TK), :], preferred_element_type=F32)
        return m_new, l, acc

    _, l, acc = lax.fori_loop(0, n_lat // TK, body, (m, l, acc))
    return l, acc


def _attend_fixed(qz, kc_ref, vc_ref, kl_ref, vl_ref, n_lat, shift):
    def chunk(k, v):
        s = lax.dot_general(qz, k, _NT_DIMS, preferred_element_type=F32)
        p = jnp.exp2(s - shift)
        lp = p[:, :LANE]
        for t in range(1, p.shape[1] // LANE):
            lp = lp + p[:, t * LANE:(t + 1) * LANE]
        return lp, jnp.dot(p.astype(BF16), v, preferred_element_type=F32)

    lp, acc = chunk(kc_ref[...], vc_ref[...])
    if n_lat:
        tk = math.gcd(n_lat, TK_FIXED)
        sub = math.gcd(tk, TK_SUB)

        def body(j, carry):
            lp, acc = carry
            for u in range(tk // sub):
                off = pl.multiple_of(j * tk + u * sub, sub)
                dl, da = chunk(kl_ref[pl.ds(off, sub), :], vl_ref[pl.ds(off, sub), :])
                lp, acc = lp + dl, acc + da
            return lp, acc

        lp, acc = lax.fori_loop(0, n_lat // tk, body, (lp, acc))
    return jnp.sum(lp, axis=1, keepdims=True), acc


def _attend(qz, ctl_ref, kc_ref, vc_ref, kl_ref, vl_ref, n_lat, finish):
    fixed = ctl_ref[1] > 0.5

    @pl.when(fixed)
    def _():
        finish(*_attend_fixed(qz, kc_ref, vc_ref, kl_ref, vl_ref, n_lat, ctl_ref[0]))

    @pl.when(jnp.logical_not(fixed))
    def _():
        finish(*_attend_online(qz, kc_ref, vc_ref, kl_ref, vl_ref, n_lat))


def _da_attn_kernel(*refs, n_lat, lam_init):
    if n_lat:
        ctl_ref, q_ref, kc_ref, vc_ref, kl_ref, vl_ref, lam_ref, sub_ref, o_ref = refs
    else:
        ctl_ref, q_ref, kc_ref, vc_ref, lam_ref, sub_ref, _, o_ref = refs
        kl_ref = vl_ref = None
    q = q_ref[...]
    tq = q.shape[0]
    lane = lax.broadcasted_iota(jnp.int32, q.shape, 1)
    zero = jnp.zeros_like(q)
    qz = jnp.concatenate([jnp.where(lane < DA_HEAD, q, zero), jnp.where(lane >= DA_HEAD, q, zero)], axis=0)

    def finish(l, acc):
        o = acc / l
        lamp = lam_ref[...]
        lam = (jnp.exp(jnp.sum(lamp[0:1, :] * lamp[1:2, :], axis=1, keepdims=True))
               - jnp.exp(jnp.sum(lamp[2:3, :] * lamp[3:4, :], axis=1, keepdims=True)) + lam_init)
        d = o[:tq, :] - lam * o[tq:, :]
        ms = jnp.mean(d * d, axis=-1, keepdims=True)
        o_ref[...] = (d * lax.rsqrt(ms + NORM_EPS) * sub_ref[...] * (1.0 - lam_init)).astype(BF16)

    _attend(qz, ctl_ref, kc_ref, vc_ref, kl_ref, vl_ref, n_lat, finish)


def _mla_attn_kernel(*refs, n_lat):
    if n_lat:
        ctl_ref, q_ref, kc_ref, vc_ref, kl_ref, vl_ref, o_ref = refs
    else:
        ctl_ref, q_ref, kc_ref, vc_ref, _, o_ref = refs
        kl_ref = vl_ref = None

    def finish(l, acc):
        o_ref[...] = (acc / l).astype(BF16)

    _attend(q_ref[...], ctl_ref, kc_ref, vc_ref, kl_ref, vl_ref, n_lat, finish)


def _amax(g):
    return jnp.max(jnp.abs(g))


def _mla_norm_bound(g):
    return jnp.sqrt(MLA_NOPE * _amax(g[:MLA_NOPE]) ** 2 + MLA_ROPE * _amax(g[MLA_NOPE:]) ** 2)


def _softmax_ctl(bound):
    return jnp.stack([bound, (bound <= FAST_MAX_SHIFT).astype(F32)]).astype(F32)


def _attn_calls(kernel_fn, name, ctl, q_arr, k_arr, v_arr, extra, dq, dv, k_col0, v_col0, tq, S, B, CTX, heads,
                with_ctx):
    NT = q_arr.shape[0]
    ctx_blk0 = (B * S) // CTX
    smem = pl.BlockSpec(memory_space=pltpu.SMEM)
    extra_specs = [pl.BlockSpec(e.shape, lambda b, h, i: (0, 0)) for e in extra]
    nq = S // tq
    out_shape = jax.ShapeDtypeStruct((NT, heads * dv), BF16)
    kc = pl.BlockSpec((CTX, dq), lambda b, h, i: (ctx_blk0 + b, k_col0 // dq + h))
    vc = pl.BlockSpec((CTX, dv), lambda b, h, i: (ctx_blk0 + b, v_col0 // dv + h))
    out = pl.pallas_call(
        functools.partial(kernel_fn, n_lat=S),
        grid=(B, heads, nq),
        in_specs=[smem, pl.BlockSpec((tq, dq), lambda b, h, i: (b * nq + i, h)), kc, vc,
                  pl.BlockSpec((S, dq), lambda b, h, i: (b, k_col0 // dq + h)),
                  pl.BlockSpec((S, dv), lambda b, h, i: (b, v_col0 // dv + h))] + extra_specs,
        out_specs=pl.BlockSpec((tq, dv), lambda b, h, i: (b * nq + i, h)),
        out_shape=out_shape,
        compiler_params=_cparams(3),
        name=name,
    )(ctl, q_arr, k_arr, v_arr, k_arr, v_arr, *extra)
    if not with_ctx:
        return out
    n_in = 4 + len(extra)
    return pl.pallas_call(
        functools.partial(kernel_fn, n_lat=0),
        grid=(B, heads, 1),
        in_specs=[smem, pl.BlockSpec((CTX, dq), lambda b, h, i: (ctx_blk0 + b, h)), kc, vc] + extra_specs
                 + [pl.BlockSpec(memory_space=pl.ANY)],
        out_specs=pl.BlockSpec((CTX, dv), lambda b, h, i: (ctx_blk0 + b, h)),
        out_shape=out_shape,
        input_output_aliases={n_in: 0},
        compiler_params=_cparams(3),
        name=name + "_ctx",
    )(ctl, q_arr, k_arr, v_arr, *extra, out)


def _merge_kernel(yc_ref, od_ref, om_ref, gc_ref, gd_ref, gm_ref, wc_ref, wd_ref, wm_ref, o_ref):
    def branch(a_ref, g_ref, w_ref):
        g = g_ref[...].astype(F32)
        return (1.0 / (1.0 + jnp.exp(-g))) * jnp.dot(a_ref[...], w_ref[...], preferred_element_type=F32)

    o_ref[...] = (branch(yc_ref, gc_ref, wc_ref) + branch(od_ref, gd_ref, wd_ref)
                  + branch(om_ref, gm_ref, wm_ref)).astype(BF16)


def _merge_call(yc, od, om, P, wc, wd, wm, n_rows, D):
    tn = 1024
    act = pl.BlockSpec((TM, CONV_W), lambda j, i: (i, 0))

    def gate(k):
        return pl.BlockSpec((TM, tn), lambda j, i: (i, (COL_GATES + k * D) // tn + j))

    wsp = pl.BlockSpec((CONV_W, tn), lambda j, i: (0, j))
    return pl.pallas_call(
        _merge_kernel,
        grid=(D // tn, n_rows // TM),
        in_specs=[act, act, act, gate(0), gate(1), gate(2), wsp, wsp, wsp],
        out_specs=pl.BlockSpec((TM, tn), lambda j, i: (i, j)),
        out_shape=jax.ShapeDtypeStruct((n_rows, D), BF16),
        compiler_params=_cparams(2),
        name="branch_merge",
    )(yc, od, om, P, P, P, wc, wd, wm)


def _oproj_kernel(*refs, n_lat_tiles):
    y_ref, w_ref, *h_refs, g1_ref, n2_ref, sh2_ref, sc2_ref, ho_ref, xo_ref = refs
    h = _select_stream(h_refs, n_lat_tiles)
    h1 = h + g1_ref[...] * jnp.dot(y_ref[...], w_ref[...], preferred_element_type=F32)
    ho_ref[...] = h1
    xo_ref[...] = _norm_mod(h1, n2_ref[...], sh2_ref[...], sc2_ref[...]).astype(BF16)


def _oproj_call(y, w_o, hs, n2, modv, l, n_rows, S, B):
    D = w_o.shape[0]
    tm = TM_OPROJ
    row = pl.BlockSpec((tm, D), lambda i: (i, 0))
    return pl.pallas_call(
        functools.partial(_oproj_kernel, n_lat_tiles=hs[0].shape[0] // tm),
        grid=(n_rows // tm,),
        in_specs=[row, pl.BlockSpec((D, D), lambda i: (0, 0))] + _stream_specs(hs, tm)
                 + [_mod_spec(l, 2, D, tm, S, B),
                    pl.BlockSpec((1, D), lambda i: (0, 0)),
                    _mod_spec(l, 3, D, tm, S, B), _mod_spec(l, 4, D, tm, S, B)],
        out_specs=[row, row],
        out_shape=[jax.ShapeDtypeStruct((n_rows, D), F32), jax.ShapeDtypeStruct((n_rows, D), BF16)],
        compiler_params=_cparams(1),
        name="out_proj",
    )(y, w_o, *hs, modv, n2, modv, modv)


def _ffn_kernel(*refs, emit_xn):
    if emit_xn:
        x_ref, w1_ref, w2_ref, h_ref, g2_ref, n_ref, sh_ref, sc_ref, ho_ref, xo_ref, acc_ref = refs
    else:
        x_ref, w1_ref, w2_ref, h_ref, g2_ref, ho_ref, acc_ref = refs
    f = pl.program_id(1)

    @pl.when(f == 0)
    def _():
        acc_ref[...] = jnp.zeros_like(acc_ref)

    hid = jnp.maximum(jnp.dot(x_ref[...], w1_ref[...], preferred_element_type=F32), 0.0)
    acc_ref[...] += jnp.dot((hid * hid).astype(BF16), w2_ref[...], preferred_element_type=F32)

    @pl.when(f == pl.num_programs(1) - 1)
    def _():
        h2 = h_ref[...] + g2_ref[...] * acc_ref[...]
        ho_ref[...] = h2
        if emit_xn:
            xo_ref[...] = _norm_mod(h2, n_ref[...], sh_ref[...], sc_ref[...]).astype(BF16)


def _ffn_call(xn, w1, w2, h, modv, l, n_rows, S, B, n_next):
    D, FF = w1.shape
    tf = 1024
    emit_xn = n_next is not None
    row = pl.BlockSpec((TM, D), lambda i, f: (i, 0))
    in_specs = [row, pl.BlockSpec((D, tf), lambda i, f: (0, f)), pl.BlockSpec((tf, D), lambda i, f: (f, 0)),
                row, _mod_spec(l, 5, D, TM, S, B)]
    args = [xn, w1, w2, h, modv]
    out_specs = [row]
    out_shape = [jax.ShapeDtypeStruct((n_rows, D), F32)]
    if emit_xn:
        in_specs += [pl.BlockSpec((1, D), lambda i, f: (0, 0)),
                     _mod_spec(l + 1, 0, D, TM, S, B), _mod_spec(l + 1, 1, D, TM, S, B)]
        args += [n_next, modv, modv]
        out_specs.append(row)
        out_shape.append(jax.ShapeDtypeStruct((n_rows, D), BF16))
    res = pl.pallas_call(
        functools.partial(_ffn_kernel, emit_xn=emit_xn),
        grid=(n_rows // TM, FF // tf),
        in_specs=in_specs,
        out_specs=out_specs,
        out_shape=out_shape,
        scratch_shapes=[pltpu.VMEM((TM, D), F32)],
        compiler_params=_cparams(2),
        name="ffn",
    )(*args)
    return (res[0], res[1]) if emit_xn else (res[0], None)


def _rope_tables(B, S, CTX):
    half = DA_HEAD // 2
    inv_freq = ROPE_THETA ** (-jnp.arange(0, half, 2, dtype=F32) / half)
    s_idx = jnp.arange(S, dtype=jnp.int32)
    rows = (s_idx // GRID_W).astype(F32)
    cols = (s_idx % GRID_W).astype(F32)
    ang_r = rows[:, None] * inv_freq[None, :]
    ang_c = cols[:, None] * inv_freq[None, :]
    ang = jnp.concatenate([ang_r, ang_r, ang_c, ang_c], axis=1)
    cos, sin = jnp.cos(ang), jnp.sin(ang)
    first = (jnp.arange(DA_HEAD) % half) < (half // 2)
    sa = jnp.where(first[None, :], -sin, 0.0)
    sb = jnp.where(first[None, :], 0.0, sin)

    def full(t, ctx_val):
        t = jnp.tile(jnp.concatenate([t, t], axis=1), (B, 1))
        return jnp.concatenate([t, jnp.full((B * CTX, LANE), ctx_val, F32)], axis=0)

    return full(cos, 1.0), full(sa, 0.0), full(sb, 0.0)


def _avg_matrix(groups):
    n = LANE * ((max(s + w for s, w in groups) + LANE - 1) // LANE)
    idx = jnp.arange(n)
    m = jnp.zeros((n, n), F32)
    for s, w in groups:
        inside = (idx >= s) & (idx < s + w)
        m = m + jnp.where(inside[:, None] & inside[None, :], 1.0 / w, 0.0)
    return m.astype(BF16)


def kernel(x, c, ctx, c_ctx, w_mod, b_mod, norm1_w, norm2_w, w_in, conv_w, da_q_norm, da_k_norm, da_lambda,
           da_subln, mla_q_a_norm, w_q_b, mla_kv_a_norm, w_kv_b, mla_q_norm, mla_k_norm, w_conv_out, w_da_out,
           w_mla_out, w_o, w_mlp1, w_mlp2):
    B, S, D = x.shape
    CTX = ctx.shape[1]
    L = w_mod.shape[0]
    NL = B * S
    NT = NL + B * CTX
    assert S % TM == 0 and S % TK == 0 and S % GRID_W == 0 and (B * CTX) % TM == 0 and CTX == T_CONV
    assert COL_KROPE + LANE <= COL_GATES and D % 1024 == 0 and B + 1 <= 16

    hs = (x.reshape(NL, D), ctx.reshape(B * CTX, D))
    cin = jnp.concatenate([c, c_ctx[None, :], jnp.zeros((16 - B - 1, D), F32)], axis=0)
    modv = _mod_call(cin, w_mod, b_mod)[:, :B + 1].reshape(L, B + 1, 1, 6 * D)

    tabs = _rope_tables(B, S, CTX)
    m64x2 = _avg_matrix([(0, DA_HEAD), (DA_HEAD, DA_HEAD)])
    m256 = _avg_matrix([(0, MLA_NOPE), (MLA_NOPE, MLA_ROPE)])
    m128 = _avg_matrix([(0, MLA_NOPE)])
    m64 = _avg_matrix([(0, MLA_ROPE)])

    q_end = 3 * CONV_W + DA_HEADS * 2 * DA_HEAD
    cq_end = q_end + Q_LORA
    kv0 = cq_end + N_BRANCH * D
    dkv_end = kv0 + DA_HEADS * 2 * DA_HEAD + DA_HEADS * DA_V

    xn = _norm_mod_call(hs, norm1_w[0][None, :], modv, 0, S, B)
    for l in range(L):
        last = l == L - 1
        n_rows = NL if last else NT
        lam_init = 0.8 - 0.6 * math.exp(-0.3 * l)

        w = w_in[l].astype(BF16)
        w_perm = jnp.concatenate(
            [w[:, :q_end], w[:, kv0:dkv_end], w[:, q_end:cq_end], w[:, dkv_end:],
             jnp.zeros((D, COL_GATES - COL_KROPE - MLA_ROPE), BF16), w[:, cq_end:kv0]], axis=1)
        wqb = jnp.pad(w_q_b[l].astype(BF16).reshape(Q_LORA, MLA_HEADS, MLA_QK),
                      ((0, 0), (0, 0), (0, MLA_QK_PAD - MLA_QK))).reshape(Q_LORA, MLA_HEADS * MLA_QK_PAD)
        zpad = jnp.zeros((MLA_QK_PAD - MLA_QK,), F32)
        qg = jnp.tile(da_q_norm[l], 2)[None, :]
        kg = jnp.tile(da_k_norm[l], 2)[None, :]
        mq_g = jnp.concatenate([mla_q_norm[l], zpad])[None, :]
        mk_gn = mla_k_norm[l][None, :MLA_NOPE]
        mk_gr = jnp.concatenate([mla_k_norm[l][MLA_NOPE:], zpad])[None, :]

        P = _proj_call(xn, w_perm)
        yc = _conv_call(P, conv_w[l], n_rows, S, B, CTX)
        q_da, k_da = _da_prep_call(P, tabs, qg, kg, m64x2)
        q_m = _mla_q_call(P, mla_q_a_norm[l][None, :], wqb, mq_g, m256, tabs)
        k_m, v_m = _mla_k_call(P, mla_kv_a_norm[l][None, :], w_kv_b[l].astype(BF16), mk_gn, mk_gr, m128, m64, tabs)

        da_ctl = _softmax_ctl(DA_SCALE * LOG2E * DA_HEAD * _amax(da_q_norm[l]) * _amax(da_k_norm[l]))
        mla_ctl = _softmax_ctl(MLA_SCALE * LOG2E * _mla_norm_bound(mla_q_norm[l]) * _mla_norm_bound(mla_k_norm[l]))
        o_da = _attn_calls(functools.partial(_da_attn_kernel, lam_init=lam_init), "da_attn", da_ctl,
                           q_da, k_da, P, [da_lambda[l], da_subln[l][None, :]],
                           2 * DA_HEAD, DA_V, 0, COL_DA_V, TQ_DA, S, B, CTX, DA_HEADS, not last)
        o_m = _attn_calls(_mla_attn_kernel, "mla_attn", mla_ctl, q_m, k_m, v_m, [],
                          MLA_QK_PAD, MLA_V, 0, 0, TQ_MLA, S, B, CTX, MLA_HEADS, not last)

        y = _merge_call(yc, o_da, o_m, P, w_conv_out[l].astype(BF16), w_da_out[l].astype(BF16),
                        w_mla_out[l].astype(BF16), n_rows, D)
        h1, xn2 = _oproj_call(y, w_o[l].astype(BF16), hs, norm2_w[l][None, :], modv, l, n_rows, S, B)
        h, xn = _ffn_call(xn2, w_mlp1[l].astype(BF16), w_mlp2[l].astype(BF16), h1, modv, l, n_rows, S, B,
                          None if last else norm1_w[l + 1][None, :])
        hs = (h,)
    return h.reshape(B, S, D)
```

```python
import functools
import math

import jax
import jax.numpy as jnp
from jax import lax
from jax.experimental import pallas as pl
from jax.experimental.pallas import tpu as pltpu

F32 = jnp.float32
BF16 = jnp.bfloat16

GRID_W = 64
ROPE_THETA = 10000.0
NORM_EPS = 1e-6
CONV_W = 1024
CONV_K = 3
DA_HEADS = 8
DA_HEAD = 64
DA_V = 2 * DA_HEAD
DA_SCALE = DA_HEAD ** -0.5
MLA_HEADS = 8
MLA_NOPE = 128
MLA_ROPE = 64
MLA_V = 128
MLA_QK = MLA_NOPE + MLA_ROPE
MLA_QK_PAD = 256
Q_LORA = 512
KV_LORA = 512
MLA_SCALE = MLA_QK ** -0.5
N_BRANCH = 3
LOG2E = math.log2(math.e)
FAST_MAX_SHIFT = 50.0

LANE = 128
SUBLANE = 8

COL_CONV_B = 0
COL_CONV_C = CONV_W
COL_CONV_H = 2 * CONV_W
COL_DA_Q = 3 * CONV_W
COL_DA_K = COL_DA_Q + DA_HEADS * 2 * DA_HEAD
COL_DA_V = COL_DA_K + DA_HEADS * 2 * DA_HEAD
COL_CQ = COL_DA_V + DA_HEADS * DA_V
COL_CKV = COL_CQ + Q_LORA
COL_KROPE = COL_CKV + KV_LORA
COL_GATES = 8192

TM = 512
TM_OPROJ = 256
T_CONV = 256
TQ_DA = 1024
TQ_MLA = 2048
TK = 512
TK_FIXED = 8192
TK_SUB = 512
VMEM_LIMIT = 52 * 1024 * 1024


def _cparams(n_axes, vmem=VMEM_LIMIT):
    return pltpu.CompilerParams(dimension_semantics=("arbitrary",) * n_axes, vmem_limit_bytes=vmem)


def _norm_mod(h, g, sh, sc):
    ms = jnp.mean(h * h, axis=-1, keepdims=True)
    return (h * lax.rsqrt(ms + NORM_EPS) * g) * (1.0 + sc) + sh


def _group_ms(x, m_ref):
    return jnp.dot((x * x).astype(BF16), m_ref[...], preferred_element_type=F32)


def _rope(r, cos, sa, sb):
    return r * cos + pltpu.roll(r, LANE - 16, 1) * sa + pltpu.roll(r, 16, 1) * sb


def _mod_kernel(c_ref, w_ref, b_ref, o_ref):
    c = c_ref[...]
    s = c * (1.0 / (1.0 + jnp.exp(-c)))
    o_ref[...] = jnp.dot(s.astype(BF16), w_ref[...].astype(BF16), preferred_element_type=F32) + b_ref[...]


def _mod_call(cin, w_mod, b_mod):
    L, D, N = w_mod.shape
    tn = 1024
    return pl.pallas_call(
        _mod_kernel,
        grid=(L, N // tn),
        in_specs=[pl.BlockSpec((16, D), lambda l, j: (0, 0)),
                  pl.BlockSpec((None, D, tn), lambda l, j: (l, 0, j)),
                  pl.BlockSpec((None, 1, tn), lambda l, j: (l, 0, j))],
        out_specs=pl.BlockSpec((None, 16, tn), lambda l, j: (l, 0, j)),
        out_shape=jax.ShapeDtypeStruct((L, 16, N), F32),
        compiler_params=_cparams(2),
        name="adaln_mod",
    )(cin, w_mod, b_mod.reshape(L, 1, N))


def _mod_spec(l, k, D, tm, S, B):
    return pl.BlockSpec((None, None, 1, D), lambda i, *_: (l, jnp.minimum(i * tm // S, B), 0, k))


def _stream_specs(hs, tm):
    D = hs[0].shape[1]
    if len(hs) == 1:
        return [pl.BlockSpec((tm, D), lambda i, *_: (i, 0))]
    n0 = hs[0].shape[0] // tm
    n1 = hs[1].shape[0] // tm
    return [pl.BlockSpec((tm, D), lambda i, *_: (jnp.minimum(i, n0 - 1), 0)),
            pl.BlockSpec((tm, D), lambda i, *_: (jnp.clip(i - n0, 0, n1 - 1), 0))]


def _norm_mod_kernel(*refs, n_lat_tiles):
    *h_refs, g_ref, sh_ref, sc_ref, o_ref = refs
    h = _select_stream(h_refs, n_lat_tiles)
    o_ref[...] = _norm_mod(h, g_ref[...], sh_ref[...], sc_ref[...]).astype(BF16)


def _select_stream(h_refs, n_lat_tiles):
    if len(h_refs) == 1:
        return h_refs[0][...]
    return jnp.where(pl.program_id(0) < n_lat_tiles, h_refs[0][...], h_refs[1][...])


def _norm_mod_call(hs, g, modv, l, S, B):
    NT = sum(h.shape[0] for h in hs)
    D = hs[0].shape[1]
    return pl.pallas_call(
        functools.partial(_norm_mod_kernel, n_lat_tiles=hs[0].shape[0] // TM),
        grid=(NT // TM,),
        in_specs=_stream_specs(hs, TM) + [pl.BlockSpec((1, D), lambda i: (0, 0)),
                                           _mod_spec(l, 0, D, TM, S, B),
                                           _mod_spec(l, 1, D, TM, S, B)],
        out_specs=pl.BlockSpec((TM, D), lambda i: (i, 0)),
        out_shape=jax.ShapeDtypeStruct((NT, D), BF16),
        compiler_params=_cparams(1),
        name="norm_mod",
    )(*hs, g, modv, modv)


def _proj_kernel(x_ref, w_ref, o_ref):
    o_ref[...] = jnp.dot(x_ref[...], w_ref[...], preferred_element_type=F32).astype(BF16)


def _proj_call(xn, w):
    NT, D = xn.shape
    NP = w.shape[1]
    tn = 2048 if NP % 2048 == 0 else 1024
    return pl.pallas_call(
        _proj_kernel,
        grid=(NP // tn, NT // TM),
        in_specs=[pl.BlockSpec((TM, D), lambda j, i: (i, 0)),
                  pl.BlockSpec((D, tn), lambda j, i: (0, j))],
        out_specs=pl.BlockSpec((TM, tn), lambda j, i: (i, j)),
        out_shape=jax.ShapeDtypeStruct((NT, NP), BF16),
        compiler_params=_cparams(2),
        name="in_proj",
    )(xn, w)


def _conv_kernel(b_ref, c_ref, h_ref, cp_ref, hp_ref, cn_ref, hn_ref, w_ref, o_ref, *, starts, ends):
    i = pl.program_id(0)
    u = c_ref[...].astype(F32) * h_ref[...].astype(F32)
    up = cp_ref[SUBLANE - 1:SUBLANE, :].astype(F32) * hp_ref[SUBLANE - 1:SUBLANE, :].astype(F32)
    un = cn_ref[0:1, :].astype(F32) * hn_ref[0:1, :].astype(F32)
    is_start = functools.reduce(jnp.logical_or, [i == s for s in starts])
    is_end = functools.reduce(jnp.logical_or, [i == e for e in ends])
    up = jnp.where(is_start, 0.0, up)
    un = jnp.where(is_end, 0.0, un)
    t = u.shape[0]
    row = lax.broadcasted_iota(jnp.int32, u.shape, 0)
    u_dn = jnp.where(row == 0, up, pltpu.roll(u, 1, 0))
    u_up = jnp.where(row == t - 1, un, pltpu.roll(u, t - 1, 0))
    w = w_ref[...]
    y = w[0:1, :] * u_dn + w[1:2, :] * u + w[2:3, :] * u_up
    o_ref[...] = (b_ref[...].astype(F32) * y).astype(BF16)


def _conv_call(P, conv_w, n_rows, S, B, CTX):
    NT = P.shape[0]
    t = T_CONV
    r8 = t // SUBLANE
    cb = CONV_W // CONV_W
    starts = [b * S // t for b in range(B)] + [(B * S + b * CTX) // t for b in range(B)]
    ends = [(b + 1) * S // t - 1 for b in range(B)] + [(B * S + (b + 1) * CTX) // t - 1 for b in range(B)]
    last8 = NT // SUBLANE - 1

    def main(col):
        return pl.BlockSpec((t, CONV_W), lambda i: (i, col // CONV_W))

    def prev(col):
        return pl.BlockSpec((SUBLANE, CONV_W), lambda i: (jnp.maximum(i * r8 - 1, 0), col // CONV_W))

    def nxt(col):
        return pl.BlockSpec((SUBLANE, CONV_W), lambda i: (jnp.minimum((i + 1) * r8, last8), col // CONV_W))

    del cb
    return pl.pallas_call(
        functools.partial(_conv_kernel, starts=starts, ends=ends),
        grid=(n_rows // t,),
        in_specs=[main(COL_CONV_B), main(COL_CONV_C), main(COL_CONV_H),
                  prev(COL_CONV_C), prev(COL_CONV_H), nxt(COL_CONV_C), nxt(COL_CONV_H),
                  pl.BlockSpec((CONV_K, CONV_W), lambda i: (0, 0))],
        out_specs=pl.BlockSpec((t, CONV_W), lambda i: (i, 0)),
        out_shape=jax.ShapeDtypeStruct((n_rows, CONV_W), BF16),
        compiler_params=_cparams(1),
        name="short_conv",
    )(P, P, P, P, P, P, P, conv_w)


def _da_prep_kernel(q_ref, k_ref, cos_ref, sa_ref, sb_ref, qg_ref, kg_ref, m_ref, qo_ref, ko_ref):
    cos, sa, sb = cos_ref[...], sa_ref[...], sb_ref[...]
    for hd in range(DA_HEADS):
        sl = slice(hd * LANE, (hd + 1) * LANE)
        for src, g_ref, dst, scale in ((q_ref, qg_ref, qo_ref, DA_SCALE * LOG2E), (k_ref, kg_ref, ko_ref, 1.0)):
            x = src[:, sl].astype(F32)
            xn = x * lax.rsqrt(_group_ms(x, m_ref) + NORM_EPS) * g_ref[...]
            dst[:, sl] = (_rope(xn, cos, sa, sb) * scale).astype(BF16)


def _da_prep_call(P, tabs, qg, kg, m64x2):
    NT = P.shape[0]
    W = DA_HEADS * 2 * DA_HEAD
    tab = pl.BlockSpec((TM, LANE), lambda i: (i, 0))
    vec = pl.BlockSpec((1, LANE), lambda i: (0, 0))
    return pl.pallas_call(
        _da_prep_kernel,
        grid=(NT // TM,),
        in_specs=[pl.BlockSpec((TM, W), lambda i: (i, COL_DA_Q // W)),
                  pl.BlockSpec((TM, W), lambda i: (i, COL_DA_K // W)),
                  tab, tab, tab, vec, vec,
                  pl.BlockSpec((LANE, LANE), lambda i: (0, 0))],
        out_specs=[pl.BlockSpec((TM, W), lambda i: (i, 0)), pl.BlockSpec((TM, W), lambda i: (i, 0))],
        out_shape=[jax.ShapeDtypeStruct((NT, W), BF16), jax.ShapeDtypeStruct((NT, W), BF16)],
        compiler_params=_cparams(1),
        name="da_prep",
    )(P, P, *tabs, qg, kg, m64x2)


def _mla_q_kernel(cq_ref, an_ref, w_ref, g_ref, m_ref, cos_ref, sa_ref, sb_ref, o_ref):
    cos, sa, sb = cos_ref[...], sa_ref[...], sb_ref[...]
    x = cq_ref[...].astype(F32)
    xn = x * lax.rsqrt(jnp.mean(x * x, axis=-1, keepdims=True) + NORM_EPS) * an_ref[...]
    q = jnp.dot(xn.astype(BF16), w_ref[...], preferred_element_type=F32)
    for hd in range(MLA_HEADS):
        c0 = hd * MLA_QK_PAD
        qh = q[:, c0:c0 + MLA_QK_PAD]
        qn = qh * lax.rsqrt(_group_ms(qh, m_ref) + NORM_EPS) * g_ref[...]
        o_ref[:, c0:c0 + LANE] = (qn[:, :LANE] * (MLA_SCALE * LOG2E)).astype(BF16)
        o_ref[:, c0 + LANE:c0 + 2 * LANE] = (_rope(qn[:, LANE:], cos, sa, sb) * (MLA_SCALE * LOG2E)).astype(BF16)


def _mla_q_call(P, an, w, g, m256, tabs):
    NT = P.shape[0]
    W = MLA_HEADS * MLA_QK_PAD
    tab = pl.BlockSpec((TM, LANE), lambda i: (i, 0))
    return pl.pallas_call(
        _mla_q_kernel,
        grid=(NT // TM,),
        in_specs=[pl.BlockSpec((TM, Q_LORA), lambda i: (i, COL_CQ // Q_LORA)),
                  pl.BlockSpec((1, Q_LORA), lambda i: (0, 0)),
                  pl.BlockSpec((Q_LORA, W), lambda i: (0, 0)),
                  pl.BlockSpec((1, MLA_QK_PAD), lambda i: (0, 0)),
                  pl.BlockSpec((MLA_QK_PAD, MLA_QK_PAD), lambda i: (0, 0)),
                  tab, tab, tab],
        out_specs=pl.BlockSpec((TM, W), lambda i: (i, 0)),
        out_shape=jax.ShapeDtypeStruct((NT, W), BF16),
        compiler_params=_cparams(1),
        name="mla_q_prep",
    )(P, an, w, g, m256, *tabs)


def _mla_k_kernel(ckv_ref, kr_ref, an_ref, w_ref, gn_ref, gr_ref, m128_ref, m64_ref,
                  cos_ref, sa_ref, sb_ref, ko_ref, vo_ref):
    x = ckv_ref[...].astype(F32)
    xn = x * lax.rsqrt(jnp.mean(x * x, axis=-1, keepdims=True) + NORM_EPS) * an_ref[...]
    kv = jnp.dot(xn.astype(BF16), w_ref[...], preferred_element_type=F32)
    kr = kr_ref[...].astype(F32)
    krn = kr * lax.rsqrt(_group_ms(kr, m64_ref) + NORM_EPS) * gr_ref[...]
    krr = _rope(krn, cos_ref[...], sa_ref[...], sb_ref[...]).astype(BF16)
    for hd in range(MLA_HEADS):
        c0 = hd * (MLA_NOPE + MLA_V)
        kn = kv[:, c0:c0 + MLA_NOPE]
        knn = kn * lax.rsqrt(_group_ms(kn, m128_ref) + NORM_EPS) * gn_ref[...]
        ko_ref[:, hd * MLA_QK_PAD:hd * MLA_QK_PAD + LANE] = knn.astype(BF16)
        ko_ref[:, hd * MLA_QK_PAD + LANE:(hd + 1) * MLA_QK_PAD] = krr
        vo_ref[:, hd * MLA_V:(hd + 1) * MLA_V] = kv[:, c0 + MLA_NOPE:c0 + MLA_NOPE + MLA_V].astype(BF16)


def _mla_k_call(P, an, w, gn, gr, m128, m64, tabs):
    NT = P.shape[0]
    WK = MLA_HEADS * MLA_QK_PAD
    WV = MLA_HEADS * MLA_V
    tab = pl.BlockSpec((TM, LANE), lambda i: (i, 0))
    vec = pl.BlockSpec((1, LANE), lambda i: (0, 0))
    mat = pl.BlockSpec((LANE, LANE), lambda i: (0, 0))
    return pl.pallas_call(
        _mla_k_kernel,
        grid=(NT // TM,),
        in_specs=[pl.BlockSpec((TM, KV_LORA), lambda i: (i, COL_CKV // KV_LORA)),
                  pl.BlockSpec((TM, LANE), lambda i: (i, COL_KROPE // LANE)),
                  pl.BlockSpec((1, KV_LORA), lambda i: (0, 0)),
                  pl.BlockSpec((KV_LORA, MLA_HEADS * (MLA_NOPE + MLA_V)), lambda i: (0, 0)),
                  vec, vec, mat, mat, tab, tab, tab],
        out_specs=[pl.BlockSpec((TM, WK), lambda i: (i, 0)), pl.BlockSpec((TM, WV), lambda i: (i, 0))],
        out_shape=[jax.ShapeDtypeStruct((NT, WK), BF16), jax.ShapeDtypeStruct((NT, WV), BF16)],
        compiler_params=_cparams(1),
        name="mla_k_prep",
    )(P, P, an, w, gn, gr, m128, m64, *tabs)


_NT_DIMS = (((1,), (1,)), ((), ()))


def _attend_online(qz, kc_ref, vc_ref, kl_ref, vl_ref, n_lat):
    s = lax.dot_general(qz, kc_ref[...], _NT_DIMS, preferred_element_type=F32)
    m = jnp.max(s, axis=1, keepdims=True)
    p = jnp.exp2(s - m)
    l = jnp.sum(p, axis=1, keepdims=True)
    acc = jnp.dot(p.astype(BF16), vc_ref[...], preferred_element_type=F32)
    if n_lat == 0:
        return l, acc

    def body(j, carry):
        m, l, acc = carry
        off = pl.multiple_of(j * TK, TK)
        s = lax.dot_general(qz, kl_ref[pl.ds(off, TK), :], _NT_DIMS, preferred_element_type=F32)
        m_new = jnp.maximum(m, jnp.max(s, axis=1, keepdims=True))
        alpha = jnp.exp2(m - m_new)
        p = jnp.exp2(s - m_new)
        l = alpha * l + jnp.sum(p, axis=1, keepdims=True)
        acc = alpha * acc + jnp.dot(p.astype(BF16), vl_ref[pl.ds(off, TK), :], preferred_element_type=F32)
        return m_new, l, acc

    _, l, acc = lax.fori_loop(0, n_lat // TK, body, (m, l, acc))
    return l, acc


def _attend_fixed(qz, kc_ref, vc_ref, kl_ref, vl_ref, n_lat, shift):
    def chunk(k, v):
        s = lax.dot_general(qz, k, _NT_DIMS, preferred_element_type=F32)
        p = jnp.exp2(s - shift)
        lp = p[:, :LANE]
        for t in range(1, p.shape[1] // LANE):
            lp = lp + p[:, t * LANE:(t + 1) * LANE]
        return lp, jnp.dot(p.astype(BF16), v, preferred_element_type=F32)

    lp, acc = chunk(kc_ref[...], vc_ref[...])
    if n_lat:
        tk = math.gcd(n_lat, TK_FIXED)
        sub = math.gcd(tk, TK_SUB)

        def body(j, carry):
            lp, acc = carry
            for u in range(tk // sub):
                off = pl.multiple_of(j * tk + u * sub, sub)
                dl, da = chunk(kl_ref[pl.ds(off, sub), :], vl_ref[pl.ds(off, sub), :])
                lp, acc = lp + dl, acc + da
            return lp, acc

        lp, acc = lax.fori_loop(0, n_lat // tk, body, (lp, acc))
    return jnp.sum(lp, axis=1, keepdims=True), acc


def _attend(qz, ctl_ref, kc_ref, vc_ref, kl_ref, vl_ref, n_lat, finish):
    fixed = ctl_ref[1] > 0.5

    @pl.when(fixed)
    def _():
        finish(*_attend_fixed(qz, kc_ref, vc_ref, kl_ref, vl_ref, n_lat, ctl_ref[0]))

    @pl.when(jnp.logical_not(fixed))
    def _():
        finish(*_attend_online(qz, kc_ref, vc_ref, kl_ref, vl_ref, n_lat))


def _da_attn_kernel(*refs, n_lat, lam_init):
    if n_lat:
        ctl_ref, q_ref, kc_ref, vc_ref, kl_ref, vl_ref, lam_ref, sub_ref, o_ref = refs
    else:
        ctl_ref, q_ref, kc_ref, vc_ref, lam_ref, sub_ref, o_ref = refs
        kl_ref = vl_ref = None
    q = q_ref[...]
    tq = q.shape[0]
    lane = lax.broadcasted_iota(jnp.int32, q.shape, 1)
    zero = jnp.zeros_like(q)
    qz = jnp.concatenate([jnp.where(lane < DA_HEAD, q, zero), jnp.where(lane >= DA_HEAD, q, zero)], axis=0)

    def finish(l, acc):
        o = acc / l
        lamp = lam_ref[...]
        lam = (jnp.exp(jnp.sum(lamp[0:1, :] * lamp[1:2, :], axis=1, keepdims=True))
               - jnp.exp(jnp.sum(lamp[2:3, :] * lamp[3:4, :], axis=1, keepdims=True)) + lam_init)
        d = o[:tq, :] - lam * o[tq:, :]
        ms = jnp.mean(d * d, axis=-1, keepdims=True)
        o_ref[...] = (d * lax.rsqrt(ms + NORM_EPS) * sub_ref[...] * (1.0 - lam_init)).astype(BF16)

    _attend(qz, ctl_ref, kc_ref, vc_ref, kl_ref, vl_ref, n_lat, finish)


def _mla_attn_kernel(*refs, n_lat):
    if n_lat:
        ctl_ref, q_ref, kc_ref, vc_ref, kl_ref, vl_ref, o_ref = refs
    else:
        ctl_ref, q_ref, kc_ref, vc_ref, o_ref = refs
        kl_ref = vl_ref = None

    def finish(l, acc):
        o_ref[...] = (acc / l).astype(BF16)

    _attend(q_ref[...], ctl_ref, kc_ref, vc_ref, kl_ref, vl_ref, n_lat, finish)


def _amax(g):
    return jnp.max(jnp.abs(g))


def _mla_norm_bound(g):
    return jnp.sqrt(MLA_NOPE * _amax(g[:MLA_NOPE]) ** 2 + MLA_ROPE * _amax(g[MLA_NOPE:]) ** 2)


def _softmax_ctl(bound):
    return jnp.stack([bound, (bound <= FAST_MAX_SHIFT).astype(F32)]).astype(F32)


def _attn_calls(kernel_fn, name, ctl, q_arr, k_arr, v_arr, extra, dq, dv, k_col0, v_col0, tq, S, B, CTX, heads,
                with_ctx):
    ctx_blk0 = (B * S) // CTX
    smem = pl.BlockSpec(memory_space=pltpu.SMEM)
    extra_specs = [pl.BlockSpec(e.shape, lambda b, h, i: (0, 0)) for e in extra]
    tq = math.gcd(tq, S)
    nq = S // tq
    kc = pl.BlockSpec((CTX, dq), lambda b, h, i: (ctx_blk0 + b, k_col0 // dq + h))
    vc = pl.BlockSpec((CTX, dv), lambda b, h, i: (ctx_blk0 + b, v_col0 // dv + h))
    out = pl.pallas_call(
        functools.partial(kernel_fn, n_lat=S),
        grid=(B, heads, nq),
        in_specs=[smem, pl.BlockSpec((tq, dq), lambda b, h, i: (b * nq + i, h)), kc, vc,
                  pl.BlockSpec((S, dq), lambda b, h, i: (b, k_col0 // dq + h)),
                  pl.BlockSpec((S, dv), lambda b, h, i: (b, v_col0 // dv + h))] + extra_specs,
        out_specs=pl.BlockSpec((tq, dv), lambda b, h, i: (b * nq + i, h)),
        out_shape=jax.ShapeDtypeStruct((B * S, heads * dv), BF16),
        compiler_params=_cparams(3),
        name=name,
    )(ctl, q_arr, k_arr, v_arr, k_arr, v_arr, *extra)
    if not with_ctx:
        return (out,)
    out_ctx = pl.pallas_call(
        functools.partial(kernel_fn, n_lat=0),
        grid=(B, heads, 1),
        in_specs=[smem, pl.BlockSpec((CTX, dq), lambda b, h, i: (ctx_blk0 + b, h)), kc, vc] + extra_specs,
        out_specs=pl.BlockSpec((CTX, dv), lambda b, h, i: (b, h)),
        out_shape=jax.ShapeDtypeStruct((B * CTX, heads * dv), BF16),
        compiler_params=_cparams(3),
        name=name + "_ctx",
    )(ctl, q_arr, k_arr, v_arr, *extra)
    return out, out_ctx


def _mixer_out_kernel(*refs, n_parts, n_lat_tiles):
    yc_ref, *rest = refs
    od_refs, om_refs, rest = rest[:n_parts], rest[n_parts:2 * n_parts], rest[2 * n_parts:]
    gc_ref, gd_ref, gm_ref, wc_ref, wd_ref, wm_ref, wo_ref, *rest = rest
    h_refs, rest = rest[:n_parts], rest[n_parts:]
    g1_ref, n2_ref, sh2_ref, sc2_ref, ho_ref, xo_ref = rest

    def branch(a, g_ref, w_ref):
        g = g_ref[...].astype(F32)
        return (1.0 / (1.0 + jnp.exp(-g))) * jnp.dot(a, w_ref[...], preferred_element_type=F32)

    y = (branch(yc_ref[...], gc_ref, wc_ref) + branch(_select_stream(od_refs, n_lat_tiles), gd_ref, wd_ref)
         + branch(_select_stream(om_refs, n_lat_tiles), gm_ref, wm_ref)).astype(BF16)
    h = _select_stream(h_refs, n_lat_tiles)
    h1 = h + g1_ref[...] * jnp.dot(y, wo_ref[...], preferred_element_type=F32)
    ho_ref[...] = h1
    xo_ref[...] = _norm_mod(h1, n2_ref[...], sh2_ref[...], sc2_ref[...]).astype(BF16)


def _mixer_out_call(yc, ods, oms, P, wc, wd, wm, w_o, hs, n2, modv, l, n_rows, S, B):
    D = w_o.shape[0]
    tm = TM_OPROJ
    assert len(ods) == len(oms) == len(hs)
    row = pl.BlockSpec((tm, D), lambda i: (i, 0))

    def gate(k):
        return pl.BlockSpec((tm, D), lambda i: (i, (COL_GATES + k * D) // D))

    def resident(shape):
        return pl.BlockSpec(shape, lambda i: (0, 0), pipeline_mode=pl.Buffered(1))

    return pl.pallas_call(
        functools.partial(_mixer_out_kernel, n_parts=len(hs), n_lat_tiles=hs[0].shape[0] // tm),
        grid=(n_rows // tm,),
        in_specs=[pl.BlockSpec((tm, CONV_W), lambda i: (i, 0))] + _stream_specs(ods, tm) + _stream_specs(oms, tm)
                 + [gate(0), gate(1), gate(2)]
                 + [resident((CONV_W, D)), resident((DA_HEADS * DA_V, D)), resident((MLA_HEADS * MLA_V, D)),
                    resident((D, D))]
                 + _stream_specs(hs, tm)
                 + [_mod_spec(l, 2, D, tm, S, B),
                    pl.BlockSpec((1, D), lambda i: (0, 0)),
                    _mod_spec(l, 3, D, tm, S, B), _mod_spec(l, 4, D, tm, S, B)],
        out_specs=[row, row],
        out_shape=[jax.ShapeDtypeStruct((n_rows, D), F32), jax.ShapeDtypeStruct((n_rows, D), BF16)],
        compiler_params=_cparams(1),
        name="mixer_out",
    )(yc, *ods, *oms, P, P, P, wc, wd, wm, w_o, *hs, modv, n2, modv, modv)


def _ffn_kernel(*refs, emit_xn):
    if emit_xn:
        x_ref, w1_ref, w2_ref, h_ref, g2_ref, n_ref, sh_ref, sc_ref, ho_ref, xo_ref, acc_ref = refs
    else:
        x_ref, w1_ref, w2_ref, h_ref, g2_ref, ho_ref, acc_ref = refs
    f = pl.program_id(1)

    @pl.when(f == 0)
    def _():
        acc_ref[...] = jnp.zeros_like(acc_ref)

    hid = jnp.maximum(jnp.dot(x_ref[...], w1_ref[...], preferred_element_type=F32), 0.0)
    acc_ref[...] += jnp.dot((hid * hid).astype(BF16), w2_ref[...], preferred_element_type=F32)

    @pl.when(f == pl.num_programs(1) - 1)
    def _():
        h2 = h_ref[...] + g2_ref[...] * acc_ref[...]
        ho_ref[...] = h2
        if emit_xn:
            xo_ref[...] = _norm_mod(h2, n_ref[...], sh_ref[...], sc_ref[...]).astype(BF16)


def _ffn_call(xn, w1, w2, h, modv, l, n_rows, S, B, n_next):
    D, FF = w1.shape
    tf = 1024
    emit_xn = n_next is not None
    row = pl.BlockSpec((TM, D), lambda i, f: (i, 0))
    in_specs = [row, pl.BlockSpec((D, tf), lambda i, f: (0, f)), pl.BlockSpec((tf, D), lambda i, f: (f, 0)),
                row, _mod_spec(l, 5, D, TM, S, B)]
    args = [xn, w1, w2, h, modv]
    out_specs = [row]
    out_shape = [jax.ShapeDtypeStruct((n_rows, D), F32)]
    if emit_xn:
        in_specs += [pl.BlockSpec((1, D), lambda i, f: (0, 0)),
                     _mod_spec(l + 1, 0, D, TM, S, B), _mod_spec(l + 1, 1, D, TM, S, B)]
        args += [n_next, modv, modv]
        out_specs.append(row)
        out_shape.append(jax.ShapeDtypeStruct((n_rows, D), BF16))
    res = pl.pallas_call(
        functools.partial(_ffn_kernel, emit_xn=emit_xn),
        grid=(n_rows // TM, FF // tf),
        in_specs=in_specs,
        out_specs=out_specs,
        out_shape=out_shape,
        scratch_shapes=[pltpu.VMEM((TM, D), F32)],
        compiler_params=_cparams(2),
        name="ffn",
    )(*args)
    return (res[0], res[1]) if emit_xn else (res[0], None)


def _rope_tables(B, S, CTX):
    half = DA_HEAD // 2
    inv_freq = ROPE_THETA ** (-jnp.arange(0, half, 2, dtype=F32) / half)
    s_idx = jnp.arange(S, dtype=jnp.int32)
    rows = (s_idx // GRID_W).astype(F32)
    cols = (s_idx % GRID_W).astype(F32)
    ang_r = rows[:, None] * inv_freq[None, :]
    ang_c = cols[:, None] * inv_freq[None, :]
    ang = jnp.concatenate([ang_r, ang_r, ang_c, ang_c], axis=1)
    cos, sin = jnp.cos(ang), jnp.sin(ang)
    first = (jnp.arange(DA_HEAD) % half) < (half // 2)
    sa = jnp.where(first[None, :], -sin, 0.0)
    sb = jnp.where(first[None, :], 0.0, sin)

    def full(t, ctx_val):
        t = jnp.tile(jnp.concatenate([t, t], axis=1), (B, 1))
        return jnp.concatenate([t, jnp.full((B * CTX, LANE), ctx_val, F32)], axis=0)

    return full(cos, 1.0), full(sa, 0.0), full(sb, 0.0)


def _avg_matrix(groups):
    n = LANE * ((max(s + w for s, w in groups) + LANE - 1) // LANE)
    idx = jnp.arange(n)
    m = jnp.zeros((n, n), F32)
    for s, w in groups:
        inside = (idx >= s) & (idx < s + w)
        m = m + jnp.where(inside[:, None] & inside[None, :], 1.0 / w, 0.0)
    return m.astype(BF16)


def kernel(x, c, ctx, c_ctx, w_mod, b_mod, norm1_w, norm2_w, w_in, conv_w, da_q_norm, da_k_norm, da_lambda,
           da_subln, mla_q_a_norm, w_q_b, mla_kv_a_norm, w_kv_b, mla_q_norm, mla_k_norm, w_conv_out, w_da_out,
           w_mla_out, w_o, w_mlp1, w_mlp2):
    B, S, D = x.shape
    CTX = ctx.shape[1]
    L = w_mod.shape[0]
    NL = B * S
    NT = NL + B * CTX
    assert S % TM == 0 and S % TK == 0 and S % GRID_W == 0 and (B * CTX) % TM == 0 and CTX == T_CONV
    assert COL_KROPE + LANE <= COL_GATES and D % 1024 == 0 and B + 1 <= 16

    hs = (x.reshape(NL, D), ctx.reshape(B * CTX, D))
    cin = jnp.concatenate([c, c_ctx[None, :], jnp.zeros((16 - B - 1, D), F32)], axis=0)
    modv = _mod_call(cin, w_mod, b_mod)[:, :B + 1].reshape(L, B + 1, 1, 6 * D)

    tabs = _rope_tables(B, S, CTX)
    m64x2 = _avg_matrix([(0, DA_HEAD), (DA_HEAD, DA_HEAD)])
    m256 = _avg_matrix([(0, MLA_NOPE), (MLA_NOPE, MLA_ROPE)])
    m128 = _avg_matrix([(0, MLA_NOPE)])
    m64 = _avg_matrix([(0, MLA_ROPE)])

    q_end = 3 * CONV_W + DA_HEADS * 2 * DA_HEAD
    cq_end = q_end + Q_LORA
    kv0 = cq_end + N_BRANCH * D
    dkv_end = kv0 + DA_HEADS * 2 * DA_HEAD + DA_HEADS * DA_V

    xn = _norm_mod_call(hs, norm1_w[0][None, :], modv, 0, S, B)
    for l in range(L):
        last = l == L - 1
        n_rows = NL if last else NT
        lam_init = 0.8 - 0.6 * math.exp(-0.3 * l)

        w = w_in[l].astype(BF16)
        w_perm = jnp.concatenate(
            [w[:, :q_end], w[:, kv0:dkv_end], w[:, q_end:cq_end], w[:, dkv_end:],
             jnp.zeros((D, COL_GATES - COL_KROPE - MLA_ROPE), BF16), w[:, cq_end:kv0]], axis=1)
        wqb = jnp.pad(w_q_b[l].astype(BF16).reshape(Q_LORA, MLA_HEADS, MLA_QK),
                      ((0, 0), (0, 0), (0, MLA_QK_PAD - MLA_QK))).reshape(Q_LORA, MLA_HEADS * MLA_QK_PAD)
        zpad = jnp.zeros((MLA_QK_PAD - MLA_QK,), F32)
        qg = jnp.tile(da_q_norm[l], 2)[None, :]
        kg = jnp.tile(da_k_norm[l], 2)[None, :]
        mq_g = jnp.concatenate([mla_q_norm[l], zpad])[None, :]
        mk_gn = mla_k_norm[l][None, :MLA_NOPE]
        mk_gr = jnp.concatenate([mla_k_norm[l][MLA_NOPE:], zpad])[None, :]

        P = _proj_call(xn, w_perm)
        yc = _conv_call(P, conv_w[l], n_rows, S, B, CTX)
        q_da, k_da = _da_prep_call(P, tabs, qg, kg, m64x2)
        q_m = _mla_q_call(P, mla_q_a_norm[l][None, :], wqb, mq_g, m256, tabs)
        k_m, v_m = _mla_k_call(P, mla_kv_a_norm[l][None, :], w_kv_b[l].astype(BF16), mk_gn, mk_gr, m128, m64, tabs)

        da_ctl = _softmax_ctl(DA_SCALE * LOG2E * DA_HEAD * _amax(da_q_norm[l]) * _amax(da_k_norm[l]))
        mla_ctl = _softmax_ctl(MLA_SCALE * LOG2E * _mla_norm_bound(mla_q_norm[l]) * _mla_norm_bound(mla_k_norm[l]))
        o_da = _attn_calls(functools.partial(_da_attn_kernel, lam_init=lam_init), "da_attn", da_ctl,
                           q_da, k_da, P, [da_lambda[l], da_subln[l][None, :]],
                           2 * DA_HEAD, DA_V, 0, COL_DA_V, TQ_DA, S, B, CTX, DA_HEADS, not last)
        o_m = _attn_calls(_mla_attn_kernel, "mla_attn", mla_ctl, q_m, k_m, v_m, [],
                          MLA_QK_PAD, MLA_V, 0, 0, TQ_MLA, S, B, CTX, MLA_HEADS, not last)

        h1, xn2 = _mixer_out_call(yc, o_da, o_m, P, w_conv_out[l].astype(BF16), w_da_out[l].astype(BF16),
                                  w_mla_out[l].astype(BF16), w_o[l].astype(BF16), hs, norm2_w[l][None, :],
                                  modv, l, n_rows, S, B)
        h, xn = _ffn_call(xn2, w_mlp1[l].astype(BF16), w_mlp2[l].astype(BF16), h1, modv, l, n_rows, S, B,
                          None if last else norm1_w[l + 1][None, :])
        hs = (h,)
    return h.reshape(B, S, D)
```

```python
import functools
import math

import jax
import jax.numpy as jnp
from jax import lax
from jax.experimental import pallas as pl
from jax.experimental.pallas import tpu as pltpu

F32 = jnp.float32
BF16 = jnp.bfloat16

GRID_W = 64
ROPE_THETA = 10000.0
NORM_EPS = 1e-6
CONV_W = 1024
CONV_K = 3
DA_HEADS = 8
DA_HEAD = 64
DA_V = 2 * DA_HEAD
DA_SCALE = DA_HEAD ** -0.5
MLA_HEADS = 8
MLA_NOPE = 128
MLA_ROPE = 64
MLA_V = 128
MLA_QK = MLA_NOPE + MLA_ROPE
MLA_QK_PAD = 256
Q_LORA = 512
KV_LORA = 512
MLA_SCALE = MLA_QK ** -0.5
N_BRANCH = 3
LOG2E = math.log2(math.e)
FAST_MAX_SHIFT = 50.0

LANE = 128
SUBLANE = 8

COL_CONV_B = 0
COL_CONV_C = CONV_W
COL_CONV_H = 2 * CONV_W
COL_DA_Q = 3 * CONV_W
COL_DA_K = COL_DA_Q + DA_HEADS * 2 * DA_HEAD
COL_DA_V = COL_DA_K + DA_HEADS * 2 * DA_HEAD
COL_GATES = COL_DA_V + DA_HEADS * DA_V
MLA_IN_COLS = Q_LORA + KV_LORA + LANE

TM = 512
TM_OPROJ = 256
T_CONV = 256
TQ_DA = 1024
TQ_MLA = 2048
TK = 512
TK_FIXED = 8192
TK_SUB = 512
VMEM_LIMIT = 52 * 1024 * 1024


def _cparams(n_axes, vmem=VMEM_LIMIT):
    return pltpu.CompilerParams(dimension_semantics=("arbitrary",) * n_axes, vmem_limit_bytes=vmem)


def _norm_mod(h, g, sh, sc):
    ms = jnp.mean(h * h, axis=-1, keepdims=True)
    return (h * lax.rsqrt(ms + NORM_EPS) * g) * (1.0 + sc) + sh


def _group_ms(x, m_ref):
    return jnp.dot((x * x).astype(BF16), m_ref[...], preferred_element_type=F32)


def _rope(r, cos, sa, sb):
    return r * cos + pltpu.roll(r, LANE - 16, 1) * sa + pltpu.roll(r, 16, 1) * sb


def _mod_kernel(c_ref, w_ref, b_ref, o_ref):
    c = c_ref[...]
    s = c * (1.0 / (1.0 + jnp.exp(-c)))
    o_ref[...] = jnp.dot(s.astype(BF16), w_ref[...].astype(BF16), preferred_element_type=F32) + b_ref[...]


def _mod_call(cin, w_mod, b_mod):
    L, D, N = w_mod.shape
    tn = 1024
    return pl.pallas_call(
        _mod_kernel,
        grid=(L, N // tn),
        in_specs=[pl.BlockSpec((16, D), lambda l, j: (0, 0)),
                  pl.BlockSpec((None, D, tn), lambda l, j: (l, 0, j)),
                  pl.BlockSpec((None, 1, tn), lambda l, j: (l, 0, j))],
        out_specs=pl.BlockSpec((None, 16, tn), lambda l, j: (l, 0, j)),
        out_shape=jax.ShapeDtypeStruct((L, 16, N), F32),
        compiler_params=_cparams(2),
        name="adaln_mod",
    )(cin, w_mod, b_mod.reshape(L, 1, N))


def _mod_spec(l, k, D, tm, S, B):
    return pl.BlockSpec((None, None, 1, D), lambda i, *_: (l, jnp.minimum(i * tm // S, B), 0, k))


def _stream_specs(hs, tm):
    D = hs[0].shape[1]
    if len(hs) == 1:
        return [pl.BlockSpec((tm, D), lambda i, *_: (i, 0))]
    n0 = hs[0].shape[0] // tm
    n1 = hs[1].shape[0] // tm
    return [pl.BlockSpec((tm, D), lambda i, *_: (jnp.minimum(i, n0 - 1), 0)),
            pl.BlockSpec((tm, D), lambda i, *_: (jnp.clip(i - n0, 0, n1 - 1), 0))]


def _norm_mod_kernel(*refs, n_lat_tiles):
    *h_refs, g_ref, sh_ref, sc_ref, o_ref = refs
    h = _select_stream(h_refs, n_lat_tiles)
    o_ref[...] = _norm_mod(h, g_ref[...], sh_ref[...], sc_ref[...]).astype(BF16)


def _select_stream(h_refs, n_lat_tiles):
    if len(h_refs) == 1:
        return h_refs[0][...]
    return jnp.where(pl.program_id(0) < n_lat_tiles, h_refs[0][...], h_refs[1][...])


def _norm_mod_call(hs, g, modv, l, S, B):
    NT = sum(h.shape[0] for h in hs)
    D = hs[0].shape[1]
    return pl.pallas_call(
        functools.partial(_norm_mod_kernel, n_lat_tiles=hs[0].shape[0] // TM),
        grid=(NT // TM,),
        in_specs=_stream_specs(hs, TM) + [pl.BlockSpec((1, D), lambda i: (0, 0)),
                                           _mod_spec(l, 0, D, TM, S, B),
                                           _mod_spec(l, 1, D, TM, S, B)],
        out_specs=pl.BlockSpec((TM, D), lambda i: (i, 0)),
        out_shape=jax.ShapeDtypeStruct((NT, D), BF16),
        compiler_params=_cparams(1),
        name="norm_mod",
    )(*hs, g, modv, modv)


def _proj_kernel(x_ref, w_ref, o_ref):
    o_ref[...] = jnp.dot(x_ref[...], w_ref[...], preferred_element_type=F32).astype(BF16)


def _proj_call(xn, w):
    NT, D = xn.shape
    NP = w.shape[1]
    tn = 2048 if NP % 2048 == 0 else 1024
    return pl.pallas_call(
        _proj_kernel,
        grid=(NP // tn, NT // TM),
        in_specs=[pl.BlockSpec((TM, D), lambda j, i: (i, 0)),
                  pl.BlockSpec((D, tn), lambda j, i: (0, j))],
        out_specs=pl.BlockSpec((TM, tn), lambda j, i: (i, j)),
        out_shape=jax.ShapeDtypeStruct((NT, NP), BF16),
        compiler_params=_cparams(2),
        name="in_proj",
    )(xn, w)


def _conv_kernel(b_ref, c_ref, h_ref, cp_ref, hp_ref, cn_ref, hn_ref, w_ref, o_ref, *, starts, ends):
    i = pl.program_id(0)
    u = c_ref[...].astype(F32) * h_ref[...].astype(F32)
    up = cp_ref[SUBLANE - 1:SUBLANE, :].astype(F32) * hp_ref[SUBLANE - 1:SUBLANE, :].astype(F32)
    un = cn_ref[0:1, :].astype(F32) * hn_ref[0:1, :].astype(F32)
    is_start = functools.reduce(jnp.logical_or, [i == s for s in starts])
    is_end = functools.reduce(jnp.logical_or, [i == e for e in ends])
    up = jnp.where(is_start, 0.0, up)
    un = jnp.where(is_end, 0.0, un)
    t = u.shape[0]
    row = lax.broadcasted_iota(jnp.int32, u.shape, 0)
    u_dn = jnp.where(row == 0, up, pltpu.roll(u, 1, 0))
    u_up = jnp.where(row == t - 1, un, pltpu.roll(u, t - 1, 0))
    w = w_ref[...]
    y = w[0:1, :] * u_dn + w[1:2, :] * u + w[2:3, :] * u_up
    o_ref[...] = (b_ref[...].astype(F32) * y).astype(BF16)


def _conv_call(P, conv_w, n_rows, S, B, CTX):
    NT = P.shape[0]
    t = T_CONV
    r8 = t // SUBLANE
    cb = CONV_W // CONV_W
    starts = [b * S // t for b in range(B)] + [(B * S + b * CTX) // t for b in range(B)]
    ends = [(b + 1) * S // t - 1 for b in range(B)] + [(B * S + (b + 1) * CTX) // t - 1 for b in range(B)]
    last8 = NT // SUBLANE - 1

    def main(col):
        return pl.BlockSpec((t, CONV_W), lambda i: (i, col // CONV_W))

    def prev(col):
        return pl.BlockSpec((SUBLANE, CONV_W), lambda i: (jnp.maximum(i * r8 - 1, 0), col // CONV_W))

    def nxt(col):
        return pl.BlockSpec((SUBLANE, CONV_W), lambda i: (jnp.minimum((i + 1) * r8, last8), col // CONV_W))

    del cb
    return pl.pallas_call(
        functools.partial(_conv_kernel, starts=starts, ends=ends),
        grid=(n_rows // t,),
        in_specs=[main(COL_CONV_B), main(COL_CONV_C), main(COL_CONV_H),
                  prev(COL_CONV_C), prev(COL_CONV_H), nxt(COL_CONV_C), nxt(COL_CONV_H),
                  pl.BlockSpec((CONV_K, CONV_W), lambda i: (0, 0))],
        out_specs=pl.BlockSpec((t, CONV_W), lambda i: (i, 0)),
        out_shape=jax.ShapeDtypeStruct((n_rows, CONV_W), BF16),
        compiler_params=_cparams(1),
        name="short_conv",
    )(P, P, P, P, P, P, P, conv_w)


def _da_prep_kernel(q_ref, k_ref, cos_ref, sa_ref, sb_ref, qg_ref, kg_ref, m_ref, qo_ref, ko_ref):
    cos, sa, sb = cos_ref[...], sa_ref[...], sb_ref[...]
    for hd in range(DA_HEADS):
        sl = slice(hd * LANE, (hd + 1) * LANE)
        for src, g_ref, dst, scale in ((q_ref, qg_ref, qo_ref, DA_SCALE * LOG2E), (k_ref, kg_ref, ko_ref, 1.0)):
            x = src[:, sl].astype(F32)
            xn = x * lax.rsqrt(_group_ms(x, m_ref) + NORM_EPS) * g_ref[...]
            dst[:, sl] = (_rope(xn, cos, sa, sb) * scale).astype(BF16)


def _da_prep_call(P, tabs, qg, kg, m64x2):
    NT = P.shape[0]
    W = DA_HEADS * 2 * DA_HEAD
    tab = pl.BlockSpec((TM, LANE), lambda i: (i, 0))
    vec = pl.BlockSpec((1, LANE), lambda i: (0, 0))
    return pl.pallas_call(
        _da_prep_kernel,
        grid=(NT // TM,),
        in_specs=[pl.BlockSpec((TM, W), lambda i: (i, COL_DA_Q // W)),
                  pl.BlockSpec((TM, W), lambda i: (i, COL_DA_K // W)),
                  tab, tab, tab, vec, vec,
                  pl.BlockSpec((LANE, LANE), lambda i: (0, 0))],
        out_specs=[pl.BlockSpec((TM, W), lambda i: (i, 0)), pl.BlockSpec((TM, W), lambda i: (i, 0))],
        out_shape=[jax.ShapeDtypeStruct((NT, W), BF16), jax.ShapeDtypeStruct((NT, W), BF16)],
        compiler_params=_cparams(1),
        name="da_prep",
    )(P, P, *tabs, qg, kg, m64x2)


def _mla_prep_kernel(x_ref, win_ref, qan_ref, wq_ref, qg_ref, m256_ref, kan_ref, wkv_ref, gn_ref, gr_ref,
                     m128_ref, m64_ref, cos_ref, sa_ref, sb_ref, qo_ref, ko_ref, vo_ref):
    cos, sa, sb = cos_ref[...], sa_ref[...], sb_ref[...]
    lo = jnp.dot(x_ref[...], win_ref[...], preferred_element_type=F32)

    def rms(x, gain):
        return (x * lax.rsqrt(jnp.mean(x * x, axis=-1, keepdims=True) + NORM_EPS) * gain).astype(BF16)

    q = jnp.dot(rms(lo[:, :Q_LORA], qan_ref[...]), wq_ref[...], preferred_element_type=F32)
    for hd in range(MLA_HEADS):
        c0 = hd * MLA_QK_PAD
        qh = q[:, c0:c0 + MLA_QK_PAD]
        qn = qh * lax.rsqrt(_group_ms(qh, m256_ref) + NORM_EPS) * qg_ref[...]
        qo_ref[:, c0:c0 + LANE] = (qn[:, :LANE] * (MLA_SCALE * LOG2E)).astype(BF16)
        qo_ref[:, c0 + LANE:c0 + 2 * LANE] = (_rope(qn[:, LANE:], cos, sa, sb) * (MLA_SCALE * LOG2E)).astype(BF16)

    kv = jnp.dot(rms(lo[:, Q_LORA:Q_LORA + KV_LORA], kan_ref[...]), wkv_ref[...],
                 preferred_element_type=F32)
    kr = lo[:, Q_LORA + KV_LORA:]
    krn = kr * lax.rsqrt(_group_ms(kr, m64_ref) + NORM_EPS) * gr_ref[...]
    krr = _rope(krn, cos, sa, sb).astype(BF16)
    for hd in range(MLA_HEADS):
        c0 = hd * (MLA_NOPE + MLA_V)
        kn = kv[:, c0:c0 + MLA_NOPE]
        knn = kn * lax.rsqrt(_group_ms(kn, m128_ref) + NORM_EPS) * gn_ref[...]
        ko_ref[:, hd * MLA_QK_PAD:hd * MLA_QK_PAD + LANE] = knn.astype(BF16)
        ko_ref[:, hd * MLA_QK_PAD + LANE:(hd + 1) * MLA_QK_PAD] = krr
        vo_ref[:, hd * MLA_V:(hd + 1) * MLA_V] = kv[:, c0 + MLA_NOPE:c0 + MLA_NOPE + MLA_V].astype(BF16)


def _mla_prep_call(xn, w_in_mla, qan, wq, qg, m256, kan, wkv, gn, gr, m128, m64, tabs):
    NT, D = xn.shape
    WQ = MLA_HEADS * MLA_QK_PAD
    WV = MLA_HEADS * MLA_V

    def const(a):
        return pl.BlockSpec(a.shape, lambda i: (0, 0))

    tab = pl.BlockSpec((TM, LANE), lambda i: (i, 0))
    row = lambda w: pl.BlockSpec((TM, w), lambda i: (i, 0))
    consts = [w_in_mla, qan, wq, qg, m256, kan, wkv, gn, gr, m128, m64]
    return pl.pallas_call(
        _mla_prep_kernel,
        grid=(NT // TM,),
        in_specs=[row(D)] + [const(a) for a in consts] + [tab, tab, tab],
        out_specs=[row(WQ), row(WQ), row(WV)],
        out_shape=[jax.ShapeDtypeStruct((NT, WQ), BF16), jax.ShapeDtypeStruct((NT, WQ), BF16),
                   jax.ShapeDtypeStruct((NT, WV), BF16)],
        compiler_params=_cparams(1),
        name="mla_prep",
    )(xn, *consts, *tabs)


_NT_DIMS = (((1,), (1,)), ((), ()))


def _attend_online(qz, kc_ref, vc_ref, kl_ref, vl_ref, n_lat):
    s = lax.dot_general(qz, kc_ref[...], _NT_DIMS, preferred_element_type=F32)
    m = jnp.max(s, axis=1, keepdims=True)
    p = jnp.exp2(s - m)
    l = jnp.sum(p, axis=1, keepdims=True)
    acc = jnp.dot(p.astype(BF16), vc_ref[...], preferred_element_type=F32)
    if n_lat == 0:
        return l, acc

    def body(j, carry):
        m, l, acc = carry
        off = pl.multiple_of(j * TK, TK)
        s = lax.dot_general(qz, kl_ref[pl.ds(off, TK), :], _NT_DIMS, preferred_element_type=F32)
        m_new = jnp.maximum(m, jnp.max(s, axis=1, keepdims=True))
        alpha = jnp.exp2(m - m_new)
        p = jnp.exp2(s - m_new)
        l = alpha * l + jnp.sum(p, axis=1, keepdims=True)
        acc = alpha * acc + jnp.dot(p.astype(BF16), vl_ref[pl.ds(off, TK), :], preferred_element_type=F32)
        return m_new, l, acc

    _, l, acc = lax.fori_loop(0, n_lat // TK, body, (m, l, acc))
    return l, acc


def _attend_fixed(qz, kc_ref, vc_ref, kl_ref, vl_ref, n_lat, shift):
    def chunk(k, v):
        s = lax.dot_general(qz, k, _NT_DIMS, preferred_element_type=F32)
        p = jnp.exp2(s - shift)
        lp = p[:, :LANE]
        for t in range(1, p.shape[1] // LANE):
            lp = lp + p[:, t * LANE:(t + 1) * LANE]
        return lp, jnp.dot(p.astype(BF16), v, preferred_element_type=F32)

    lp, acc = chunk(kc_ref[...], vc_ref[...])
    if n_lat:
        tk = math.gcd(n_lat, TK_FIXED)
        sub = math.gcd(tk, TK_SUB)

        def body(j, carry):
            lp, acc = carry
            for u in range(tk // sub):
                off = pl.multiple_of(j * tk + u * sub, sub)
                dl, da = chunk(kl_ref[pl.ds(off, sub), :], vl_ref[pl.ds(off, sub), :])
                lp, acc = lp + dl, acc + da
            return lp, acc

        lp, acc = lax.fori_loop(0, n_lat // tk, body, (lp, acc))
    return jnp.sum(lp, axis=1, keepdims=True), acc


def _attend(qz, ctl_ref, kc_ref, vc_ref, kl_ref, vl_ref, n_lat, finish):
    fixed = ctl_ref[1] > 0.5

    @pl.when(fixed)
    def _():
        finish(*_attend_fixed(qz, kc_ref, vc_ref, kl_ref, vl_ref, n_lat, ctl_ref[0]))

    @pl.when(jnp.logical_not(fixed))
    def _():
        finish(*_attend_online(qz, kc_ref, vc_ref, kl_ref, vl_ref, n_lat))


def _da_attn_kernel(*refs, n_lat, lam_init):
    if n_lat:
        ctl_ref, q_ref, kc_ref, vc_ref, kl_ref, vl_ref, lam_ref, sub_ref, o_ref = refs
    else:
        ctl_ref, q_ref, kc_ref, vc_ref, lam_ref, sub_ref, o_ref = refs
        kl_ref = vl_ref = None
    q = q_ref[...]
    tq = q.shape[0]
    lane = lax.broadcasted_iota(jnp.int32, q.shape, 1)
    zero = jnp.zeros_like(q)
    qz = jnp.concatenate([jnp.where(lane < DA_HEAD, q, zero), jnp.where(lane >= DA_HEAD, q, zero)], axis=0)

    def finish(l, acc):
        o = acc / l
        lamp = lam_ref[...]
        lam = (jnp.exp(jnp.sum(lamp[0:1, :] * lamp[1:2, :], axis=1, keepdims=True))
               - jnp.exp(jnp.sum(lamp[2:3, :] * lamp[3:4, :], axis=1, keepdims=True)) + lam_init)
        d = o[:tq, :] - lam * o[tq:, :]
        ms = jnp.mean(d * d, axis=-1, keepdims=True)
        o_ref[...] = (d * lax.rsqrt(ms + NORM_EPS) * sub_ref[...] * (1.0 - lam_init)).astype(BF16)

    _attend(qz, ctl_ref, kc_ref, vc_ref, kl_ref, vl_ref, n_lat, finish)


def _mla_attn_kernel(*refs, n_lat):
    if n_lat:
        ctl_ref, q_ref, kc_ref, vc_ref, kl_ref, vl_ref, o_ref = refs
    else:
        ctl_ref, q_ref, kc_ref, vc_ref, o_ref = refs
        kl_ref = vl_ref = None

    def finish(l, acc):
        o_ref[...] = (acc / l).astype(BF16)

    _attend(q_ref[...], ctl_ref, kc_ref, vc_ref, kl_ref, vl_ref, n_lat, finish)


def _amax(g):
    return jnp.max(jnp.abs(g))


def _mla_norm_bound(g):
    return jnp.sqrt(MLA_NOPE * _amax(g[:MLA_NOPE]) ** 2 + MLA_ROPE * _amax(g[MLA_NOPE:]) ** 2)


def _softmax_ctl(bound):
    return jnp.stack([bound, (bound <= FAST_MAX_SHIFT).astype(F32)]).astype(F32)


def _attn_calls(kernel_fn, name, ctl, q_arr, k_arr, v_arr, extra, dq, dv, k_col0, v_col0, tq, S, B, CTX, heads,
                with_ctx):
    ctx_blk0 = (B * S) // CTX
    smem = pl.BlockSpec(memory_space=pltpu.SMEM)
    extra_specs = [pl.BlockSpec(e.shape, lambda b, h, i: (0, 0)) for e in extra]
    tq = math.gcd(tq, S)
    nq = S // tq
    kc = pl.BlockSpec((CTX, dq), lambda b, h, i: (ctx_blk0 + b, k_col0 // dq + h))
    vc = pl.BlockSpec((CTX, dv), lambda b, h, i: (ctx_blk0 + b, v_col0 // dv + h))
    out = pl.pallas_call(
        functools.partial(kernel_fn, n_lat=S),
        grid=(B, heads, nq),
        in_specs=[smem, pl.BlockSpec((tq, dq), lambda b, h, i: (b * nq + i, h)), kc, vc,
                  pl.BlockSpec((S, dq), lambda b, h, i: (b, k_col0 // dq + h)),
                  pl.BlockSpec((S, dv), lambda b, h, i: (b, v_col0 // dv + h))] + extra_specs,
        out_specs=pl.BlockSpec((tq, dv), lambda b, h, i: (b * nq + i, h)),
        out_shape=jax.ShapeDtypeStruct((B * S, heads * dv), BF16),
        compiler_params=_cparams(3),
        name=name,
    )(ctl, q_arr, k_arr, v_arr, k_arr, v_arr, *extra)
    if not with_ctx:
        return (out,)
    out_ctx = pl.pallas_call(
        functools.partial(kernel_fn, n_lat=0),
        grid=(B, heads, 1),
        in_specs=[smem, pl.BlockSpec((CTX, dq), lambda b, h, i: (ctx_blk0 + b, h)), kc, vc] + extra_specs,
        out_specs=pl.BlockSpec((CTX, dv), lambda b, h, i: (b, h)),
        out_shape=jax.ShapeDtypeStruct((B * CTX, heads * dv), BF16),
        compiler_params=_cparams(3),
        name=name + "_ctx",
    )(ctl, q_arr, k_arr, v_arr, *extra)
    return out, out_ctx


def _mixer_out_kernel(*refs, n_parts, n_lat_tiles):
    yc_ref, *rest = refs
    od_refs, om_refs, rest = rest[:n_parts], rest[n_parts:2 * n_parts], rest[2 * n_parts:]
    gc_ref, gd_ref, gm_ref, wc_ref, wd_ref, wm_ref, wo_ref, *rest = rest
    h_refs, rest = rest[:n_parts], rest[n_parts:]
    g1_ref, n2_ref, sh2_ref, sc2_ref, ho_ref, xo_ref = rest

    def branch(a, g_ref, w_ref):
        g = g_ref[...].astype(F32)
        return (1.0 / (1.0 + jnp.exp(-g))) * jnp.dot(a, w_ref[...], preferred_element_type=F32)

    y = (branch(yc_ref[...], gc_ref, wc_ref) + branch(_select_stream(od_refs, n_lat_tiles), gd_ref, wd_ref)
         + branch(_select_stream(om_refs, n_lat_tiles), gm_ref, wm_ref)).astype(BF16)
    h = _select_stream(h_refs, n_lat_tiles)
    h1 = h + g1_ref[...] * jnp.dot(y, wo_ref[...], preferred_element_type=F32)
    ho_ref[...] = h1
    xo_ref[...] = _norm_mod(h1, n2_ref[...], sh2_ref[...], sc2_ref[...]).astype(BF16)


def _mixer_out_call(yc, ods, oms, P, wc, wd, wm, w_o, hs, n2, modv, l, n_rows, S, B):
    D = w_o.shape[0]
    tm = TM_OPROJ
    assert len(ods) == len(oms) == len(hs)
    row = pl.BlockSpec((tm, D), lambda i: (i, 0))

    def gate(k):
        return pl.BlockSpec((tm, D), lambda i: (i, (COL_GATES + k * D) // D))

    def resident(shape):
        return pl.BlockSpec(shape, lambda i: (0, 0), pipeline_mode=pl.Buffered(1))

    return pl.pallas_call(
        functools.partial(_mixer_out_kernel, n_parts=len(hs), n_lat_tiles=hs[0].shape[0] // tm),
        grid=(n_rows // tm,),
        in_specs=[pl.BlockSpec((tm, CONV_W), lambda i: (i, 0))] + _stream_specs(ods, tm) + _stream_specs(oms, tm)
                 + [gate(0), gate(1), gate(2)]
                 + [resident((CONV_W, D)), resident((DA_HEADS * DA_V, D)), resident((MLA_HEADS * MLA_V, D)),
                    resident((D, D))]
                 + _stream_specs(hs, tm)
                 + [_mod_spec(l, 2, D, tm, S, B),
                    pl.BlockSpec((1, D), lambda i: (0, 0)),
                    _mod_spec(l, 3, D, tm, S, B), _mod_spec(l, 4, D, tm, S, B)],
        out_specs=[row, row],
        out_shape=[jax.ShapeDtypeStruct((n_rows, D), F32), jax.ShapeDtypeStruct((n_rows, D), BF16)],
        compiler_params=_cparams(1),
        name="mixer_out",
    )(yc, *ods, *oms, P, P, P, wc, wd, wm, w_o, *hs, modv, n2, modv, modv)


def _ffn_kernel(*refs, emit_xn):
    if emit_xn:
        x_ref, w1_ref, w2_ref, h_ref, g2_ref, n_ref, sh_ref, sc_ref, ho_ref, xo_ref, acc_ref = refs
    else:
        x_ref, w1_ref, w2_ref, h_ref, g2_ref, ho_ref, acc_ref = refs
    f = pl.program_id(1)

    @pl.when(f == 0)
    def _():
        acc_ref[...] = jnp.zeros_like(acc_ref)

    hid = jnp.maximum(jnp.dot(x_ref[...], w1_ref[...], preferred_element_type=F32), 0.0)
    acc_ref[...] += jnp.dot((hid * hid).astype(BF16), w2_ref[...], preferred_element_type=F32)

    @pl.when(f == pl.num_programs(1) - 1)
    def _():
        h2 = h_ref[...] + g2_ref[...] * acc_ref[...]
        ho_ref[...] = h2
        if emit_xn:
            xo_ref[...] = _norm_mod(h2, n_ref[...], sh_ref[...], sc_ref[...]).astype(BF16)


def _ffn_call(xn, w1, w2, h, modv, l, n_rows, S, B, n_next):
    D, FF = w1.shape
    tf = 1024
    emit_xn = n_next is not None
    row = pl.BlockSpec((TM, D), lambda i, f: (i, 0))
    in_specs = [row, pl.BlockSpec((D, tf), lambda i, f: (0, f)), pl.BlockSpec((tf, D), lambda i, f: (f, 0)),
                row, _mod_spec(l, 5, D, TM, S, B)]
    args = [xn, w1, w2, h, modv]
    out_specs = [row]
    out_shape = [jax.ShapeDtypeStruct((n_rows, D), F32)]
    if emit_xn:
        in_specs += [pl.BlockSpec((1, D), lambda i, f: (0, 0)),
                     _mod_spec(l + 1, 0, D, TM, S, B), _mod_spec(l + 1, 1, D, TM, S, B)]
        args += [n_next, modv, modv]
        out_specs.append(row)
        out_shape.append(jax.ShapeDtypeStruct((n_rows, D), BF16))
    res = pl.pallas_call(
        functools.partial(_ffn_kernel, emit_xn=emit_xn),
        grid=(n_rows // TM, FF // tf),
        in_specs=in_specs,
        out_specs=out_specs,
        out_shape=out_shape,
        scratch_shapes=[pltpu.VMEM((TM, D), F32)],
        compiler_params=_cparams(2),
        name="ffn",
    )(*args)
    return (res[0], res[1]) if emit_xn else (res[0], None)


def _rope_tables(B, S, CTX):
    half = DA_HEAD // 2
    inv_freq = ROPE_THETA ** (-jnp.arange(0, half, 2, dtype=F32) / half)
    s_idx = jnp.arange(S, dtype=jnp.int32)
    rows = (s_idx // GRID_W).astype(F32)
    cols = (s_idx % GRID_W).astype(F32)
    ang_r = rows[:, None] * inv_freq[None, :]
    ang_c = cols[:, None] * inv_freq[None, :]
    ang = jnp.concatenate([ang_r, ang_r, ang_c, ang_c], axis=1)
    cos, sin = jnp.cos(ang), jnp.sin(ang)
    first = (jnp.arange(DA_HEAD) % half) < (half // 2)
    sa = jnp.where(first[None, :], -sin, 0.0)
    sb = jnp.where(first[None, :], 0.0, sin)

    def full(t, ctx_val):
        t = jnp.tile(jnp.concatenate([t, t], axis=1), (B, 1))
        return jnp.concatenate([t, jnp.full((B * CTX, LANE), ctx_val, F32)], axis=0)

    return full(cos, 1.0), full(sa, 0.0), full(sb, 0.0)


def _avg_matrix(groups):
    n = LANE * ((max(s + w for s, w in groups) + LANE - 1) // LANE)
    idx = jnp.arange(n)
    m = jnp.zeros((n, n), F32)
    for s, w in groups:
        inside = (idx >= s) & (idx < s + w)
        m = m + jnp.where(inside[:, None] & inside[None, :], 1.0 / w, 0.0)
    return m.astype(BF16)


def kernel(x, c, ctx, c_ctx, w_mod, b_mod, norm1_w, norm2_w, w_in, conv_w, da_q_norm, da_k_norm, da_lambda,
           da_subln, mla_q_a_norm, w_q_b, mla_kv_a_norm, w_kv_b, mla_q_norm, mla_k_norm, w_conv_out, w_da_out,
           w_mla_out, w_o, w_mlp1, w_mlp2):
    B, S, D = x.shape
    CTX = ctx.shape[1]
    L = w_mod.shape[0]
    NL = B * S
    NT = NL + B * CTX
    assert S % TM == 0 and S % TK == 0 and S % GRID_W == 0 and (B * CTX) % TM == 0 and CTX == T_CONV
    assert D % 1024 == 0 and B + 1 <= 16

    hs = (x.reshape(NL, D), ctx.reshape(B * CTX, D))
    cin = jnp.concatenate([c, c_ctx[None, :], jnp.zeros((16 - B - 1, D), F32)], axis=0)
    modv = _mod_call(cin, w_mod, b_mod)[:, :B + 1].reshape(L, B + 1, 1, 6 * D)

    tabs = _rope_tables(B, S, CTX)
    m64x2 = _avg_matrix([(0, DA_HEAD), (DA_HEAD, DA_HEAD)])
    m256 = _avg_matrix([(0, MLA_NOPE), (MLA_NOPE, MLA_ROPE)])
    m128 = _avg_matrix([(0, MLA_NOPE)])
    m64 = _avg_matrix([(0, MLA_ROPE)])

    q_end = 3 * CONV_W + DA_HEADS * 2 * DA_HEAD
    cq_end = q_end + Q_LORA
    kv0 = cq_end + N_BRANCH * D
    dkv_end = kv0 + DA_HEADS * 2 * DA_HEAD + DA_HEADS * DA_V

    xn = _norm_mod_call(hs, norm1_w[0][None, :], modv, 0, S, B)
    for l in range(L):
        last = l == L - 1
        n_rows = NL if last else NT
        lam_init = 0.8 - 0.6 * math.exp(-0.3 * l)

        w = w_in[l].astype(BF16)
        w_perm = jnp.concatenate([w[:, :q_end], w[:, kv0:dkv_end], w[:, cq_end:kv0]], axis=1)
        w_in_mla = jnp.concatenate([w[:, q_end:cq_end], w[:, dkv_end:],
                                    jnp.zeros((D, LANE - MLA_ROPE), BF16)], axis=1)
        wqb = jnp.pad(w_q_b[l].astype(BF16).reshape(Q_LORA, MLA_HEADS, MLA_QK),
                      ((0, 0), (0, 0), (0, MLA_QK_PAD - MLA_QK))).reshape(Q_LORA, MLA_HEADS * MLA_QK_PAD)
        zpad = jnp.zeros((MLA_QK_PAD - MLA_QK,), F32)
        qg = jnp.tile(da_q_norm[l], 2)[None, :]
        kg = jnp.tile(da_k_norm[l], 2)[None, :]
        mq_g = jnp.concatenate([mla_q_norm[l], zpad])[None, :]
        mk_gn = mla_k_norm[l][None, :MLA_NOPE]
        mk_gr = jnp.concatenate([mla_k_norm[l][MLA_NOPE:], zpad])[None, :]

        P = _proj_call(xn, w_perm)
        yc = _conv_call(P, conv_w[l], n_rows, S, B, CTX)
        q_da, k_da = _da_prep_call(P, tabs, qg, kg, m64x2)
        q_m, k_m, v_m = _mla_prep_call(xn, w_in_mla, mla_q_a_norm[l][None, :], wqb, mq_g, m256,
                                       mla_kv_a_norm[l][None, :], w_kv_b[l].astype(BF16), mk_gn, mk_gr, m128, m64,
                                       tabs)

        da_ctl = _softmax_ctl(DA_SCALE * LOG2E * DA_HEAD * _amax(da_q_norm[l]) * _amax(da_k_norm[l]))
        mla_ctl = _softmax_ctl(MLA_SCALE * LOG2E * _mla_norm_bound(mla_q_norm[l]) * _mla_norm_bound(mla_k_norm[l]))
        o_da = _attn_calls(functools.partial(_da_attn_kernel, lam_init=lam_init), "da_attn", da_ctl,
                           q_da, k_da, P, [da_lambda[l], da_subln[l][None, :]],
                           2 * DA_HEAD, DA_V, 0, COL_DA_V, TQ_DA, S, B, CTX, DA_HEADS, not last)
        o_m = _attn_calls(_mla_attn_kernel, "mla_attn", mla_ctl, q_m, k_m, v_m, [],
                          MLA_QK_PAD, MLA_V, 0, 0, TQ_MLA, S, B, CTX, MLA_HEADS, not last)

        h1, xn2 = _mixer_out_call(yc, o_da, o_m, P, w_conv_out[l].astype(BF16), w_da_out[l].astype(BF16),
                                  w_mla_out[l].astype(BF16), w_o[l].astype(BF16), hs, norm2_w[l][None, :],
                                  modv, l, n_rows, S, B)
        h, xn = _ffn_call(xn2, w_mlp1[l].astype(BF16), w_mlp2[l].astype(BF16), h1, modv, l, n_rows, S, B,
                          None if last else norm1_w[l + 1][None, :])
        hs = (h,)
    return h.reshape(B, S, D)
```

```python
import functools
import math

import jax
import jax.numpy as jnp
from jax import lax
from jax.experimental import pallas as pl
from jax.experimental.pallas import tpu as pltpu

F32 = jnp.float32
BF16 = jnp.bfloat16

GRID_W = 64
ROPE_THETA = 10000.0
NORM_EPS = 1e-6
CONV_W = 1024
CONV_K = 3
DA_HEADS = 8
DA_HEAD = 64
DA_V = 2 * DA_HEAD
DA_SCALE = DA_HEAD ** -0.5
MLA_HEADS = 8
MLA_NOPE = 128
MLA_ROPE = 64
MLA_V = 128
MLA_QK = MLA_NOPE + MLA_ROPE
MLA_QK_PAD = 256
Q_LORA = 512
KV_LORA = 512
MLA_SCALE = MLA_QK ** -0.5
N_BRANCH = 3
LOG2E = math.log2(math.e)
FAST_MAX_SHIFT = 50.0

LANE = 128
SUBLANE = 8

COL_CONV_B = 0
COL_CONV_C = CONV_W
COL_CONV_H = 2 * CONV_W
COL_DA_Q = 3 * CONV_W
COL_DA_K = COL_DA_Q + DA_HEADS * 2 * DA_HEAD
COL_DA_V = COL_DA_K + DA_HEADS * 2 * DA_HEAD
COL_GATES = COL_DA_V + DA_HEADS * DA_V
MLA_IN_COLS = Q_LORA + KV_LORA + LANE

TM = 512
TM_OPROJ = 256
TQ_DA = 1024
TQ_MLA = 2048
TK = 512
TK_FIXED = 8192
TK_SUB = 512
VMEM_LIMIT = 52 * 1024 * 1024


def _cparams(n_axes, vmem=VMEM_LIMIT):
    return pltpu.CompilerParams(dimension_semantics=("arbitrary",) * n_axes, vmem_limit_bytes=vmem)


def _norm_mod(h, g, sh, sc):
    ms = jnp.mean(h * h, axis=-1, keepdims=True)
    return (h * lax.rsqrt(ms + NORM_EPS) * g) * (1.0 + sc) + sh


def _group_ms(x, m_ref):
    return jnp.dot((x * x).astype(BF16), m_ref[...], preferred_element_type=F32)


def _rope(r, cos, sa, sb):
    return r * cos + pltpu.roll(r, LANE - 16, 1) * sa + pltpu.roll(r, 16, 1) * sb


def _mod_kernel(c_ref, w_ref, b_ref, o_ref):
    c = c_ref[...]
    s = c * (1.0 / (1.0 + jnp.exp(-c)))
    o_ref[...] = jnp.dot(s.astype(BF16), w_ref[...].astype(BF16), preferred_element_type=F32) + b_ref[...]


def _mod_call(cin, w_mod, b_mod):
    L, D, N = w_mod.shape
    tn = 1024
    return pl.pallas_call(
        _mod_kernel,
        grid=(L, N // tn),
        in_specs=[pl.BlockSpec((16, D), lambda l, j: (0, 0)),
                  pl.BlockSpec((None, D, tn), lambda l, j: (l, 0, j)),
                  pl.BlockSpec((None, 1, tn), lambda l, j: (l, 0, j))],
        out_specs=pl.BlockSpec((None, 16, tn), lambda l, j: (l, 0, j)),
        out_shape=jax.ShapeDtypeStruct((L, 16, N), F32),
        compiler_params=_cparams(2),
        name="adaln_mod",
    )(cin, w_mod, b_mod.reshape(L, 1, N))


def _mod_spec(l, k, D, tm, S, B):
    return pl.BlockSpec((None, None, 1, D), lambda i, *_: (l, jnp.minimum(i * tm // S, B), 0, k))


def _stream_specs(hs, tm):
    D = hs[0].shape[1]
    if len(hs) == 1:
        return [pl.BlockSpec((tm, D), lambda i, *_: (i, 0))]
    n0 = hs[0].shape[0] // tm
    n1 = hs[1].shape[0] // tm
    return [pl.BlockSpec((tm, D), lambda i, *_: (jnp.minimum(i, n0 - 1), 0)),
            pl.BlockSpec((tm, D), lambda i, *_: (jnp.clip(i - n0, 0, n1 - 1), 0))]


def _norm_mod_kernel(*refs, n_lat_tiles):
    *h_refs, g_ref, sh_ref, sc_ref, o_ref = refs
    h = _select_stream(h_refs, n_lat_tiles)
    o_ref[...] = _norm_mod(h, g_ref[...], sh_ref[...], sc_ref[...]).astype(BF16)


def _select_stream(h_refs, n_lat_tiles):
    if len(h_refs) == 1:
        return h_refs[0][...]
    return jnp.where(pl.program_id(0) < n_lat_tiles, h_refs[0][...], h_refs[1][...])


def _norm_mod_call(hs, g, modv, l, S, B):
    NT = sum(h.shape[0] for h in hs)
    D = hs[0].shape[1]
    return pl.pallas_call(
        functools.partial(_norm_mod_kernel, n_lat_tiles=hs[0].shape[0] // TM),
        grid=(NT // TM,),
        in_specs=_stream_specs(hs, TM) + [pl.BlockSpec((1, D), lambda i: (0, 0)),
                                           _mod_spec(l, 0, D, TM, S, B),
                                           _mod_spec(l, 1, D, TM, S, B)],
        out_specs=pl.BlockSpec((TM, D), lambda i: (i, 0)),
        out_shape=jax.ShapeDtypeStruct((NT, D), BF16),
        compiler_params=_cparams(1),
        name="norm_mod",
    )(*hs, g, modv, modv)


def _proj_kernel(x_ref, w_ref, o_ref):
    o_ref[...] = jnp.dot(x_ref[...], w_ref[...], preferred_element_type=F32).astype(BF16)


def _proj_call(xn, w):
    NT, D = xn.shape
    NP = w.shape[1]
    tn = 2048 if NP % 2048 == 0 else 1024
    return pl.pallas_call(
        _proj_kernel,
        grid=(NP // tn, NT // TM),
        in_specs=[pl.BlockSpec((TM, D), lambda j, i: (i, 0)),
                  pl.BlockSpec((D, tn), lambda j, i: (0, j))],
        out_specs=pl.BlockSpec((TM, tn), lambda j, i: (i, j)),
        out_shape=jax.ShapeDtypeStruct((NT, NP), BF16),
        compiler_params=_cparams(2),
        name="in_proj",
    )(xn, w)


def _conv_tile(b_ref, c_ref, h_ref, cp_ref, hp_ref, cn_ref, hn_ref, w_ref, starts, ends):
    i = pl.program_id(0)
    u = c_ref[...].astype(F32) * h_ref[...].astype(F32)
    up = cp_ref[SUBLANE - 1:SUBLANE, :].astype(F32) * hp_ref[SUBLANE - 1:SUBLANE, :].astype(F32)
    un = cn_ref[0:1, :].astype(F32) * hn_ref[0:1, :].astype(F32)
    is_start = functools.reduce(jnp.logical_or, [i == s for s in starts])
    is_end = functools.reduce(jnp.logical_or, [i == e for e in ends])
    up = jnp.where(is_start, 0.0, up)
    un = jnp.where(is_end, 0.0, un)
    t = u.shape[0]
    row = lax.broadcasted_iota(jnp.int32, u.shape, 0)
    u_dn = jnp.where(row == 0, up, pltpu.roll(u, 1, 0))
    u_up = jnp.where(row == t - 1, un, pltpu.roll(u, t - 1, 0))
    w = w_ref[...]
    y = w[0:1, :] * u_dn + w[1:2, :] * u + w[2:3, :] * u_up
    return (b_ref[...].astype(F32) * y).astype(BF16)


def _conv_specs(NT, t, S, B, CTX):
    r8 = t // SUBLANE
    starts = [b * S // t for b in range(B)] + [(B * S + b * CTX) // t for b in range(B)]
    ends = [(b + 1) * S // t - 1 for b in range(B)] + [(B * S + (b + 1) * CTX) // t - 1 for b in range(B)]
    last8 = NT // SUBLANE - 1

    def main(col):
        return pl.BlockSpec((t, CONV_W), lambda i: (i, col // CONV_W))

    def prev(col):
        return pl.BlockSpec((SUBLANE, CONV_W), lambda i: (jnp.maximum(i * r8 - 1, 0), col // CONV_W))

    def nxt(col):
        return pl.BlockSpec((SUBLANE, CONV_W), lambda i: (jnp.minimum((i + 1) * r8, last8), col // CONV_W))

    specs = [main(COL_CONV_B), main(COL_CONV_C), main(COL_CONV_H),
             prev(COL_CONV_C), prev(COL_CONV_H), nxt(COL_CONV_C), nxt(COL_CONV_H)]
    return specs, starts, ends


def _da_prep_kernel(q_ref, k_ref, cos_ref, sa_ref, sb_ref, qg_ref, kg_ref, m_ref, qo_ref, ko_ref):
    cos, sa, sb = cos_ref[...], sa_ref[...], sb_ref[...]
    for hd in range(DA_HEADS):
        sl = slice(hd * LANE, (hd + 1) * LANE)
        for src, g_ref, dst, scale in ((q_ref, qg_ref, qo_ref, DA_SCALE * LOG2E), (k_ref, kg_ref, ko_ref, 1.0)):
            x = src[:, sl].astype(F32)
            xn = x * lax.rsqrt(_group_ms(x, m_ref) + NORM_EPS) * g_ref[...]
            dst[:, sl] = (_rope(xn, cos, sa, sb) * scale).astype(BF16)


def _da_prep_call(P, tabs, qg, kg, m64x2):
    NT = P.shape[0]
    W = DA_HEADS * 2 * DA_HEAD
    tab = pl.BlockSpec((TM, LANE), lambda i: (i, 0))
    vec = pl.BlockSpec((1, LANE), lambda i: (0, 0))
    return pl.pallas_call(
        _da_prep_kernel,
        grid=(NT // TM,),
        in_specs=[pl.BlockSpec((TM, W), lambda i: (i, COL_DA_Q // W)),
                  pl.BlockSpec((TM, W), lambda i: (i, COL_DA_K // W)),
                  tab, tab, tab, vec, vec,
                  pl.BlockSpec((LANE, LANE), lambda i: (0, 0))],
        out_specs=[pl.BlockSpec((TM, W), lambda i: (i, 0)), pl.BlockSpec((TM, W), lambda i: (i, 0))],
        out_shape=[jax.ShapeDtypeStruct((NT, W), BF16), jax.ShapeDtypeStruct((NT, W), BF16)],
        compiler_params=_cparams(1),
        name="da_prep",
    )(P, P, *tabs, qg, kg, m64x2)


def _mla_prep_kernel(x_ref, win_ref, qan_ref, wq_ref, qg_ref, m256_ref, kan_ref, wkv_ref, gn_ref, gr_ref,
                     m128_ref, m64_ref, cos_ref, sa_ref, sb_ref, qo_ref, ko_ref, vo_ref):
    cos, sa, sb = cos_ref[...], sa_ref[...], sb_ref[...]
    lo = jnp.dot(x_ref[...], win_ref[...], preferred_element_type=F32)

    def rms(x, gain):
        return (x * lax.rsqrt(jnp.mean(x * x, axis=-1, keepdims=True) + NORM_EPS) * gain).astype(BF16)

    q = jnp.dot(rms(lo[:, :Q_LORA], qan_ref[...]), wq_ref[...], preferred_element_type=F32)
    for hd in range(MLA_HEADS):
        c0 = hd * MLA_QK_PAD
        qh = q[:, c0:c0 + MLA_QK_PAD]
        qn = qh * lax.rsqrt(_group_ms(qh, m256_ref) + NORM_EPS) * qg_ref[...]
        qo_ref[:, c0:c0 + LANE] = (qn[:, :LANE] * (MLA_SCALE * LOG2E)).astype(BF16)
        qo_ref[:, c0 + LANE:c0 + 2 * LANE] = (_rope(qn[:, LANE:], cos, sa, sb) * (MLA_SCALE * LOG2E)).astype(BF16)

    kv = jnp.dot(rms(lo[:, Q_LORA:Q_LORA + KV_LORA], kan_ref[...]), wkv_ref[...],
                 preferred_element_type=F32)
    kr = lo[:, Q_LORA + KV_LORA:]
    krn = kr * lax.rsqrt(_group_ms(kr, m64_ref) + NORM_EPS) * gr_ref[...]
    krr = _rope(krn, cos, sa, sb).astype(BF16)
    for hd in range(MLA_HEADS):
        c0 = hd * (MLA_NOPE + MLA_V)
        kn = kv[:, c0:c0 + MLA_NOPE]
        knn = kn * lax.rsqrt(_group_ms(kn, m128_ref) + NORM_EPS) * gn_ref[...]
        ko_ref[:, hd * MLA_QK_PAD:hd * MLA_QK_PAD + LANE] = knn.astype(BF16)
        ko_ref[:, hd * MLA_QK_PAD + LANE:(hd + 1) * MLA_QK_PAD] = krr
        vo_ref[:, hd * MLA_V:(hd + 1) * MLA_V] = kv[:, c0 + MLA_NOPE:c0 + MLA_NOPE + MLA_V].astype(BF16)


def _mla_prep_call(xn, w_in_mla, qan, wq, qg, m256, kan, wkv, gn, gr, m128, m64, tabs):
    NT, D = xn.shape
    WQ = MLA_HEADS * MLA_QK_PAD
    WV = MLA_HEADS * MLA_V

    def const(a):
        return pl.BlockSpec(a.shape, lambda i: (0, 0))

    tab = pl.BlockSpec((TM, LANE), lambda i: (i, 0))
    row = lambda w: pl.BlockSpec((TM, w), lambda i: (i, 0))
    consts = [w_in_mla, qan, wq, qg, m256, kan, wkv, gn, gr, m128, m64]
    return pl.pallas_call(
        _mla_prep_kernel,
        grid=(NT // TM,),
        in_specs=[row(D)] + [const(a) for a in consts] + [tab, tab, tab],
        out_specs=[row(WQ), row(WQ), row(WV)],
        out_shape=[jax.ShapeDtypeStruct((NT, WQ), BF16), jax.ShapeDtypeStruct((NT, WQ), BF16),
                   jax.ShapeDtypeStruct((NT, WV), BF16)],
        compiler_params=_cparams(1),
        name="mla_prep",
    )(xn, *consts, *tabs)


_NT_DIMS = (((1,), (1,)), ((), ()))


def _attend_online(qz, kc_ref, vc_ref, kl_ref, vl_ref, n_lat):
    s = lax.dot_general(qz, kc_ref[...], _NT_DIMS, preferred_element_type=F32)
    m = jnp.max(s, axis=1, keepdims=True)
    p = jnp.exp2(s - m)
    l = jnp.sum(p, axis=1, keepdims=True)
    acc = jnp.dot(p.astype(BF16), vc_ref[...], preferred_element_type=F32)
    if n_lat == 0:
        return l, acc

    def body(j, carry):
        m, l, acc = carry
        off = pl.multiple_of(j * TK, TK)
        s = lax.dot_general(qz, kl_ref[pl.ds(off, TK), :], _NT_DIMS, preferred_element_type=F32)
        m_new = jnp.maximum(m, jnp.max(s, axis=1, keepdims=True))
        alpha = jnp.exp2(m - m_new)
        p = jnp.exp2(s - m_new)
        l = alpha * l + jnp.sum(p, axis=1, keepdims=True)
        acc = alpha * acc + jnp.dot(p.astype(BF16), vl_ref[pl.ds(off, TK), :], preferred_element_type=F32)
        return m_new, l, acc

    _, l, acc = lax.fori_loop(0, n_lat // TK, body, (m, l, acc))
    return l, acc


def _attend_fixed(qz, kc_ref, vc_ref, kl_ref, vl_ref, n_lat, shift):
    def chunk(k, v):
        s = lax.dot_general(qz, k, _NT_DIMS, preferred_element_type=F32)
        p = jnp.exp2(s - shift)
        lp = p[:, :LANE]
        for t in range(1, p.shape[1] // LANE):
            lp = lp + p[:, t * LANE:(t + 1) * LANE]
        return lp, jnp.dot(p.astype(BF16), v, preferred_element_type=F32)

    lp, acc = chunk(kc_ref[...], vc_ref[...])
    if n_lat:
        tk = math.gcd(n_lat, TK_FIXED)
        sub = math.gcd(tk, TK_SUB)

        def body(j, carry):
            lp, acc = carry
            for u in range(tk // sub):
                off = pl.multiple_of(j * tk + u * sub, sub)
                dl, da = chunk(kl_ref[pl.ds(off, sub), :], vl_ref[pl.ds(off, sub), :])
                lp, acc = lp + dl, acc + da
            return lp, acc

        lp, acc = lax.fori_loop(0, n_lat // tk, body, (lp, acc))
    return jnp.sum(lp, axis=1, keepdims=True), acc


def _attend(qz, ctl_ref, kc_ref, vc_ref, kl_ref, vl_ref, n_lat, finish):
    fixed = ctl_ref[1] > 0.5

    @pl.when(fixed)
    def _():
        finish(*_attend_fixed(qz, kc_ref, vc_ref, kl_ref, vl_ref, n_lat, ctl_ref[0]))

    @pl.when(jnp.logical_not(fixed))
    def _():
        finish(*_attend_online(qz, kc_ref, vc_ref, kl_ref, vl_ref, n_lat))


def _da_attn_kernel(*refs, n_lat, lam_init):
    if n_lat:
        ctl_ref, q_ref, kc_ref, vc_ref, kl_ref, vl_ref, lam_ref, sub_ref, o_ref = refs
    else:
        ctl_ref, q_ref, kc_ref, vc_ref, lam_ref, sub_ref, o_ref = refs
        kl_ref = vl_ref = None
    q = q_ref[...]
    tq = q.shape[0]
    lane = lax.broadcasted_iota(jnp.int32, q.shape, 1)
    zero = jnp.zeros_like(q)
    qz = jnp.concatenate([jnp.where(lane < DA_HEAD, q, zero), jnp.where(lane >= DA_HEAD, q, zero)], axis=0)

    def finish(l, acc):
        o = acc / l
        lamp = lam_ref[...]
        lam = (jnp.exp(jnp.sum(lamp[0:1, :] * lamp[1:2, :], axis=1, keepdims=True))
               - jnp.exp(jnp.sum(lamp[2:3, :] * lamp[3:4, :], axis=1, keepdims=True)) + lam_init)
        d = o[:tq, :] - lam * o[tq:, :]
        ms = jnp.mean(d * d, axis=-1, keepdims=True)
        o_ref[...] = (d * lax.rsqrt(ms + NORM_EPS) * sub_ref[...] * (1.0 - lam_init)).astype(BF16)

    _attend(qz, ctl_ref, kc_ref, vc_ref, kl_ref, vl_ref, n_lat, finish)


def _mla_attn_kernel(*refs, n_lat):
    if n_lat:
        ctl_ref, q_ref, kc_ref, vc_ref, kl_ref, vl_ref, o_ref = refs
    else:
        ctl_ref, q_ref, kc_ref, vc_ref, o_ref = refs
        kl_ref = vl_ref = None

    def finish(l, acc):
        o_ref[...] = (acc / l).astype(BF16)

    _attend(q_ref[...], ctl_ref, kc_ref, vc_ref, kl_ref, vl_ref, n_lat, finish)


def _amax(g):
    return jnp.max(jnp.abs(g))


def _mla_norm_bound(g):
    return jnp.sqrt(MLA_NOPE * _amax(g[:MLA_NOPE]) ** 2 + MLA_ROPE * _amax(g[MLA_NOPE:]) ** 2)


def _softmax_ctl(bound):
    return jnp.stack([bound, (bound <= FAST_MAX_SHIFT).astype(F32)]).astype(F32)


def _attn_calls(kernel_fn, name, ctl, q_arr, k_arr, v_arr, extra, dq, dv, k_col0, v_col0, tq, S, B, CTX, heads,
                with_ctx):
    ctx_blk0 = (B * S) // CTX
    smem = pl.BlockSpec(memory_space=pltpu.SMEM)
    extra_specs = [pl.BlockSpec(e.shape, lambda b, h, i: (0, 0)) for e in extra]
    tq = math.gcd(tq, S)
    nq = S // tq
    kc = pl.BlockSpec((CTX, dq), lambda b, h, i: (ctx_blk0 + b, k_col0 // dq + h))
    vc = pl.BlockSpec((CTX, dv), lambda b, h, i: (ctx_blk0 + b, v_col0 // dv + h))
    out = pl.pallas_call(
        functools.partial(kernel_fn, n_lat=S),
        grid=(B, heads, nq),
        in_specs=[smem, pl.BlockSpec((tq, dq), lambda b, h, i: (b * nq + i, h)), kc, vc,
                  pl.BlockSpec((S, dq), lambda b, h, i: (b, k_col0 // dq + h)),
                  pl.BlockSpec((S, dv), lambda b, h, i: (b, v_col0 // dv + h))] + extra_specs,
        out_specs=pl.BlockSpec((tq, dv), lambda b, h, i: (b * nq + i, h)),
        out_shape=jax.ShapeDtypeStruct((B * S, heads * dv), BF16),
        compiler_params=_cparams(3),
        name=name,
    )(ctl, q_arr, k_arr, v_arr, k_arr, v_arr, *extra)
    if not with_ctx:
        return (out,)
    out_ctx = pl.pallas_call(
        functools.partial(kernel_fn, n_lat=0),
        grid=(B, heads, 1),
        in_specs=[smem, pl.BlockSpec((CTX, dq), lambda b, h, i: (ctx_blk0 + b, h)), kc, vc] + extra_specs,
        out_specs=pl.BlockSpec((CTX, dv), lambda b, h, i: (b, h)),
        out_shape=jax.ShapeDtypeStruct((B * CTX, heads * dv), BF16),
        compiler_params=_cparams(3),
        name=name + "_ctx",
    )(ctl, q_arr, k_arr, v_arr, *extra)
    return out, out_ctx


def _mixer_out_kernel(*refs, n_parts, n_lat_tiles, starts, ends):
    conv_refs, rest = refs[:8], refs[8:]
    od_refs, om_refs, rest = rest[:n_parts], rest[n_parts:2 * n_parts], rest[2 * n_parts:]
    gc_ref, gd_ref, gm_ref, wc_ref, wd_ref, wm_ref, wo_ref, *rest = rest
    h_refs, rest = rest[:n_parts], rest[n_parts:]
    g1_ref, n2_ref, sh2_ref, sc2_ref, ho_ref, xo_ref = rest

    def branch(a, g_ref, w_ref):
        g = g_ref[...].astype(F32)
        return (1.0 / (1.0 + jnp.exp(-g))) * jnp.dot(a, w_ref[...], preferred_element_type=F32)

    yc = _conv_tile(*conv_refs, starts, ends)
    y = (branch(yc, gc_ref, wc_ref) + branch(_select_stream(od_refs, n_lat_tiles), gd_ref, wd_ref)
         + branch(_select_stream(om_refs, n_lat_tiles), gm_ref, wm_ref)).astype(BF16)
    h = _select_stream(h_refs, n_lat_tiles)
    h1 = h + g1_ref[...] * jnp.dot(y, wo_ref[...], preferred_element_type=F32)
    ho_ref[...] = h1
    xo_ref[...] = _norm_mod(h1, n2_ref[...], sh2_ref[...], sc2_ref[...]).astype(BF16)


def _mixer_out_call(conv_w, ods, oms, P, wc, wd, wm, w_o, hs, n2, modv, l, n_rows, S, B, CTX):
    D = w_o.shape[0]
    tm = TM_OPROJ
    assert len(ods) == len(oms) == len(hs)
    row = pl.BlockSpec((tm, D), lambda i: (i, 0))
    conv_specs, starts, ends = _conv_specs(P.shape[0], tm, S, B, CTX)

    def gate(k):
        return pl.BlockSpec((tm, D), lambda i: (i, (COL_GATES + k * D) // D))

    def resident(shape):
        return pl.BlockSpec(shape, lambda i: (0, 0), pipeline_mode=pl.Buffered(1))

    return pl.pallas_call(
        functools.partial(_mixer_out_kernel, n_parts=len(hs), n_lat_tiles=hs[0].shape[0] // tm,
                          starts=starts, ends=ends),
        grid=(n_rows // tm,),
        in_specs=conv_specs + [pl.BlockSpec((CONV_K, CONV_W), lambda i: (0, 0))]
                 + _stream_specs(ods, tm) + _stream_specs(oms, tm)
                 + [gate(0), gate(1), gate(2)]
                 + [resident((CONV_W, D)), resident((DA_HEADS * DA_V, D)), resident((MLA_HEADS * MLA_V, D)),
                    resident((D, D))]
                 + _stream_specs(hs, tm)
                 + [_mod_spec(l, 2, D, tm, S, B),
                    pl.BlockSpec((1, D), lambda i: (0, 0)),
                    _mod_spec(l, 3, D, tm, S, B), _mod_spec(l, 4, D, tm, S, B)],
        out_specs=[row, row],
        out_shape=[jax.ShapeDtypeStruct((n_rows, D), F32), jax.ShapeDtypeStruct((n_rows, D), BF16)],
        compiler_params=_cparams(1),
        name="mixer_out",
    )(*([P] * 7), conv_w, *ods, *oms, P, P, P, wc, wd, wm, w_o, *hs, modv, n2, modv, modv)


def _ffn_kernel(*refs, emit_xn):
    if emit_xn:
        x_ref, w1_ref, w2_ref, h_ref, g2_ref, n_ref, sh_ref, sc_ref, ho_ref, xo_ref, acc_ref = refs
    else:
        x_ref, w1_ref, w2_ref, h_ref, g2_ref, ho_ref, acc_ref = refs
    f = pl.program_id(1)

    @pl.when(f == 0)
    def _():
        acc_ref[...] = jnp.zeros_like(acc_ref)

    hid = jnp.maximum(jnp.dot(x_ref[...], w1_ref[...], preferred_element_type=F32), 0.0)
    acc_ref[...] += jnp.dot((hid * hid).astype(BF16), w2_ref[...], preferred_element_type=F32)

    @pl.when(f == pl.num_programs(1) - 1)
    def _():
        h2 = h_ref[...] + g2_ref[...] * acc_ref[...]
        ho_ref[...] = h2
        if emit_xn:
            xo_ref[...] = _norm_mod(h2, n_ref[...], sh_ref[...], sc_ref[...]).astype(BF16)


def _ffn_call(xn, w1, w2, h, modv, l, n_rows, S, B, n_next):
    D, FF = w1.shape
    tf = 1024
    emit_xn = n_next is not None
    row = pl.BlockSpec((TM, D), lambda i, f: (i, 0))
    in_specs = [row, pl.BlockSpec((D, tf), lambda i, f: (0, f)), pl.BlockSpec((tf, D), lambda i, f: (f, 0)),
                row, _mod_spec(l, 5, D, TM, S, B)]
    args = [xn, w1, w2, h, modv]
    out_specs = [row]
    out_shape = [jax.ShapeDtypeStruct((n_rows, D), F32)]
    if emit_xn:
        in_specs += [pl.BlockSpec((1, D), lambda i, f: (0, 0)),
                     _mod_spec(l + 1, 0, D, TM, S, B), _mod_spec(l + 1, 1, D, TM, S, B)]
        args += [n_next, modv, modv]
        out_specs.append(row)
        out_shape.append(jax.ShapeDtypeStruct((n_rows, D), BF16))
    res = pl.pallas_call(
        functools.partial(_ffn_kernel, emit_xn=emit_xn),
        grid=(n_rows // TM, FF // tf),
        in_specs=in_specs,
        out_specs=out_specs,
        out_shape=out_shape,
        scratch_shapes=[pltpu.VMEM((TM, D), F32)],
        compiler_params=_cparams(2),
        name="ffn",
    )(*args)
    return (res[0], res[1]) if emit_xn else (res[0], None)


def _rope_tables(B, S, CTX):
    half = DA_HEAD // 2
    inv_freq = ROPE_THETA ** (-jnp.arange(0, half, 2, dtype=F32) / half)
    s_idx = jnp.arange(S, dtype=jnp.int32)
    rows = (s_idx // GRID_W).astype(F32)
    cols = (s_idx % GRID_W).astype(F32)
    ang_r = rows[:, None] * inv_freq[None, :]
    ang_c = cols[:, None] * inv_freq[None, :]
    ang = jnp.concatenate([ang_r, ang_r, ang_c, ang_c], axis=1)
    cos, sin = jnp.cos(ang), jnp.sin(ang)
    first = (jnp.arange(DA_HEAD) % half) < (half // 2)
    sa = jnp.where(first[None, :], -sin, 0.0)
    sb = jnp.where(first[None, :], 0.0, sin)

    def full(t, ctx_val):
        t = jnp.tile(jnp.concatenate([t, t], axis=1), (B, 1))
        return jnp.concatenate([t, jnp.full((B * CTX, LANE), ctx_val, F32)], axis=0)

    return full(cos, 1.0), full(sa, 0.0), full(sb, 0.0)


def _avg_matrix(groups):
    n = LANE * ((max(s + w for s, w in groups) + LANE - 1) // LANE)
    idx = jnp.arange(n)
    m = jnp.zeros((n, n), F32)
    for s, w in groups:
        inside = (idx >= s) & (idx < s + w)
        m = m + jnp.where(inside[:, None] & inside[None, :], 1.0 / w, 0.0)
    return m.astype(BF16)


def kernel(x, c, ctx, c_ctx, w_mod, b_mod, norm1_w, norm2_w, w_in, conv_w, da_q_norm, da_k_norm, da_lambda,
           da_subln, mla_q_a_norm, w_q_b, mla_kv_a_norm, w_kv_b, mla_q_norm, mla_k_norm, w_conv_out, w_da_out,
           w_mla_out, w_o, w_mlp1, w_mlp2):
    B, S, D = x.shape
    CTX = ctx.shape[1]
    L = w_mod.shape[0]
    NL = B * S
    NT = NL + B * CTX
    assert S % TM == 0 and S % TK == 0 and S % GRID_W == 0 and (B * CTX) % TM == 0 and CTX % TM_OPROJ == 0
    assert D % 1024 == 0 and B + 1 <= 16

    hs = (x.reshape(NL, D), ctx.reshape(B * CTX, D))
    cin = jnp.concatenate([c, c_ctx[None, :], jnp.zeros((16 - B - 1, D), F32)], axis=0)
    modv = _mod_call(cin, w_mod, b_mod)[:, :B + 1].reshape(L, B + 1, 1, 6 * D)

    tabs = _rope_tables(B, S, CTX)
    m64x2 = _avg_matrix([(0, DA_HEAD), (DA_HEAD, DA_HEAD)])
    m256 = _avg_matrix([(0, MLA_NOPE), (MLA_NOPE, MLA_ROPE)])
    m128 = _avg_matrix([(0, MLA_NOPE)])
    m64 = _avg_matrix([(0, MLA_ROPE)])

    q_end = 3 * CONV_W + DA_HEADS * 2 * DA_HEAD
    cq_end = q_end + Q_LORA
    kv0 = cq_end + N_BRANCH * D
    dkv_end = kv0 + DA_HEADS * 2 * DA_HEAD + DA_HEADS * DA_V

    xn = _norm_mod_call(hs, norm1_w[0][None, :], modv, 0, S, B)
    for l in range(L):
        last = l == L - 1
        n_rows = NL if last else NT
        lam_init = 0.8 - 0.6 * math.exp(-0.3 * l)

        w = w_in[l].astype(BF16)
        w_perm = jnp.concatenate([w[:, :q_end], w[:, kv0:dkv_end], w[:, cq_end:kv0]], axis=1)
        w_in_mla = jnp.concatenate([w[:, q_end:cq_end], w[:, dkv_end:],
                                    jnp.zeros((D, LANE - MLA_ROPE), BF16)], axis=1)
        wqb = jnp.pad(w_q_b[l].astype(BF16).reshape(Q_LORA, MLA_HEADS, MLA_QK),
                      ((0, 0), (0, 0), (0, MLA_QK_PAD - MLA_QK))).reshape(Q_LORA, MLA_HEADS * MLA_QK_PAD)
        zpad = jnp.zeros((MLA_QK_PAD - MLA_QK,), F32)
        qg = jnp.tile(da_q_norm[l], 2)[None, :]
        kg = jnp.tile(da_k_norm[l], 2)[None, :]
        mq_g = jnp.concatenate([mla_q_norm[l], zpad])[None, :]
        mk_gn = mla_k_norm[l][None, :MLA_NOPE]
        mk_gr = jnp.concatenate([mla_k_norm[l][MLA_NOPE:], zpad])[None, :]

        P = _proj_call(xn, w_perm)
        q_da, k_da = _da_prep_call(P, tabs, qg, kg, m64x2)
        q_m, k_m, v_m = _mla_prep_call(xn, w_in_mla, mla_q_a_norm[l][None, :], wqb, mq_g, m256,
                                       mla_kv_a_norm[l][None, :], w_kv_b[l].astype(BF16), mk_gn, mk_gr, m128, m64,
                                       tabs)

        da_ctl = _softmax_ctl(DA_SCALE * LOG2E * DA_HEAD * _amax(da_q_norm[l]) * _amax(da_k_norm[l]))
        mla_ctl = _softmax_ctl(MLA_SCALE * LOG2E * _mla_norm_bound(mla_q_norm[l]) * _mla_norm_bound(mla_k_norm[l]))
        o_da = _attn_calls(functools.partial(_da_attn_kernel, lam_init=lam_init), "da_attn", da_ctl,
                           q_da, k_da, P, [da_lambda[l], da_subln[l][None, :]],
                           2 * DA_HEAD, DA_V, 0, COL_DA_V, TQ_DA, S, B, CTX, DA_HEADS, not last)
        o_m = _attn_calls(_mla_attn_kernel, "mla_attn", mla_ctl, q_m, k_m, v_m, [],
                          MLA_QK_PAD, MLA_V, 0, 0, TQ_MLA, S, B, CTX, MLA_HEADS, not last)

        h1, xn2 = _mixer_out_call(conv_w[l], o_da, o_m, P, w_conv_out[l].astype(BF16), w_da_out[l].astype(BF16),
                                  w_mla_out[l].astype(BF16), w_o[l].astype(BF16), hs, norm2_w[l][None, :],
                                  modv, l, n_rows, S, B, CTX)
        h, xn = _ffn_call(xn2, w_mlp1[l].astype(BF16), w_mlp2[l].astype(BF16), h1, modv, l, n_rows, S, B,
                          None if last else norm1_w[l + 1][None, :])
        hs = (h,)
    return h.reshape(B, S, D)
```
